```python
import math
import jax, jax.numpy as jnp
from jax import lax
import numpy as np

D_MODEL = 4096
BATCH = 2
SEQ = 8192
DEPTH = 4

CHUNK = 64
Q_BLOCK = 128
EPS = 1e-6
NEG_INF = -1e30

RET_HEADS = 4
RET_DK = 128
RET_DV = 256
RET_WIDTH = RET_HEADS * RET_DV
DIFF_HEADS = 4
DIFF_DH = 128
DIFF_WIDTH = DIFF_HEADS * 2 * DIFF_DH
CONV_CH = 1024
CONV_K = 31
N_BRANCH = 3
BRANCH_WIDTH = 1024
FFN_DIM = 2 * D_MODEL
FFN_CONV_K = 3

IN_SIZES = (RET_HEADS * RET_DK, RET_HEADS * RET_DK, RET_WIDTH, RET_WIDTH,
            DIFF_HEADS * 2 * DIFF_DH, DIFF_HEADS * 2 * DIFF_DH, DIFF_WIDTH,
            2 * CONV_CH)
IN_COLS = sum(IN_SIZES)
IN_SPLITS = tuple(int(s) for s in np.cumsum(IN_SIZES)[:-1])

kernel_name = "hybrid_retention_diffattn_conformer_gated"


def rms_norm(x, g):
    xf = x.astype(jnp.float32)
    y = xf * lax.rsqrt(jnp.mean(xf * xf, axis=-1, keepdims=True) + EPS)
    return (y * g.astype(jnp.float32)).astype(x.dtype)


def layer_norm(x, g, b):
    xf = x.astype(jnp.float32)
    mu = jnp.mean(xf, axis=-1, keepdims=True)
    xc = xf - mu
    y = xc * lax.rsqrt(jnp.mean(xc * xc, axis=-1, keepdims=True) + EPS)
    return (y * g.astype(jnp.float32) + b.astype(jnp.float32)).astype(x.dtype)


def causal_dwconv(x, w, b):
    k = w.shape[0]
    y = lax.conv_general_dilated(
        x, w.astype(x.dtype)[:, None, :], window_strides=(1,), padding=[(k - 1, 0)],
        dimension_numbers=('NWC', 'WIO', 'NWC'), feature_group_count=x.shape[-1])
    return y + b.astype(x.dtype)


def retention(q, k, v):
    bsz, seq, nh, dk = q.shape
    dv = v.shape[-1]
    n_chunks = seq // CHUNK
    lg = jnp.log(1.0 - 2.0 ** (-5.0 - jnp.arange(nh, dtype=jnp.float32)))
    pos = jnp.arange(CHUNK, dtype=jnp.float32)
    intra = jnp.exp(jnp.abs(pos[:, None] - pos[None, :])[None] * lg[:, None, None])
    q_dec = jnp.exp((pos[:, None] + 1.0) * lg[None, :])
    k_dec = jnp.exp((CHUNK - 1.0 - pos[:, None]) * lg[None, :])
    c_dec = jnp.exp(CHUNK * lg)

    def to_chunks(t):
        return t.reshape(bsz, n_chunks, CHUNK, nh, t.shape[-1]).transpose(1, 0, 2, 3, 4).astype(jnp.float32)

    def step(state, inp):
        qc, kc, vc = inp
        scores = jnp.einsum('bihd,bjhd->bhij', qc, kc) * intra
        inner = jnp.einsum('bhij,bjhv->bihv', scores, vc)
        cross = jnp.einsum('bihd,bhdv->bihv', qc * q_dec[None, :, :, None], state)
        state = state * c_dec[None, :, None, None] + jnp.einsum(
            'bjhd,bjhv->bhdv', kc * k_dec[None, :, :, None], vc)
        return state, inner + cross

    s0 = jnp.zeros((bsz, nh, dk, dv), jnp.float32)
    _, out = lax.scan(step, s0, (to_chunks(q), to_chunks(k), to_chunks(v)))
    return out.transpose(1, 0, 2, 3, 4).reshape(bsz, seq, nh, dv)


def diff_attention(q, k, v, lam):
    bsz, seq, nh, _, d = q.shape
    n_blocks = seq // Q_BLOCK
    slopes = 2.0 ** (-8.0 * (jnp.arange(nh, dtype=jnp.float32) + 1.0) / nh)
    key_pos = jnp.arange(seq)
    scale = d ** -0.5
    qb = q.reshape(bsz, n_blocks, Q_BLOCK, nh, 2, d).transpose(1, 0, 2, 3, 4, 5)

    def block(inp):
        qblk, bi = inp
        qpos = bi * Q_BLOCK + jnp.arange(Q_BLOCK)
        logits = jnp.einsum('bqhmd,bkhmd->bhmqk', qblk, k).astype(jnp.float32) * scale
        dist = jnp.abs(qpos[:, None] - key_pos[None, :]).astype(jnp.float32)
        allowed = (key_pos[None, :] // CHUNK) <= (qpos[:, None] // CHUNK)
        logits = logits - slopes[None, :, None, None, None] * dist
        logits = jnp.where(allowed, logits, NEG_INF)
        p = jax.nn.softmax(logits, axis=-1)
        a = p[:, :, 0] - lam * p[:, :, 1]
        return jnp.einsum('bhqk,bkhe->bqhe', a.astype(v.dtype), v)

    out = lax.map(block, (qb, jnp.arange(n_blocks)))
    return out.transpose(1, 0, 2, 3, 4).reshape(bsz, seq, nh, v.shape[-1])


def setup_inputs(seed: int = 0) -> dict:
    key = jax.random.key(seed)
    ks = jax.random.split(key, 21)

    def nrm(k, shape, scale):
        return jax.random.normal(k, shape, jnp.float32) * scale

    return {
        "x": nrm(ks[0], (BATCH, SEQ, D_MODEL), 1.0),
        "norm_mix": 1.0 + nrm(ks[1], (DEPTH, D_MODEL), 0.02),
        "w_in": nrm(ks[2], (DEPTH, D_MODEL, IN_COLS), D_MODEL ** -0.5),
        "w_gate": nrm(ks[3], (DEPTH, D_MODEL, N_BRANCH * D_MODEL), D_MODEL ** -0.5),
        "b_gate": nrm(ks[4], (DEPTH, N_BRANCH * D_MODEL), 0.02),
        "diff_q_norm": 1.0 + nrm(ks[5], (DEPTH, DIFF_DH), 0.02),
        "diff_k_norm": 1.0 + nrm(ks[6], (DEPTH, DIFF_DH), 0.02),
        "diff_lambda": nrm(ks[7], (DEPTH, 4, DIFF_DH), 0.1),
        "diff_out_norm": 1.0 + nrm(ks[8], (DEPTH, DIFF_HEADS, 2 * DIFF_DH), 0.02),
        "ret_out_norm": 1.0 + nrm(ks[9], (DEPTH, RET_HEADS, RET_DV), 0.02),
        "conv_w": nrm(ks[10], (DEPTH, CONV_K, CONV_CH), CONV_K ** -0.5),
        "conv_b": nrm(ks[11], (DEPTH, CONV_CH), 0.02),
        "conv_ln_g": 1.0 + nrm(ks[12], (DEPTH, CONV_CH), 0.02),
        "conv_ln_b": nrm(ks[13], (DEPTH, CONV_CH), 0.02),
        "w_br": nrm(ks[14], (DEPTH, N_BRANCH, BRANCH_WIDTH, D_MODEL), BRANCH_WIDTH ** -0.5),
        "w_o": nrm(ks[15], (DEPTH, D_MODEL, D_MODEL), D_MODEL ** -0.5),
        "norm_ffn": 1.0 + nrm(ks[16], (DEPTH, D_MODEL), 0.02),
        "w_ffn_in": nrm(ks[17], (DEPTH, D_MODEL, 2 * FFN_DIM), D_MODEL ** -0.5),
        "ffn_conv_w": nrm(ks[18], (DEPTH, FFN_CONV_K, FFN_DIM), FFN_CONV_K ** -0.5),
        "ffn_conv_b": nrm(ks[19], (DEPTH, FFN_DIM), 0.02),
        "w_ffn_out": nrm(ks[20], (DEPTH, FFN_DIM, D_MODEL), FFN_DIM ** -0.5),
    }


def reference(x, norm_mix, w_in, w_gate, b_gate, diff_q_norm, diff_k_norm, diff_lambda,
              diff_out_norm, ret_out_norm, conv_w, conv_b, conv_ln_g, conv_ln_b, w_br, w_o,
              norm_ffn, w_ffn_in, ffn_conv_w, ffn_conv_b, w_ffn_out):
    bsz, seq, _ = x.shape
    for l in range(DEPTH):
        h = rms_norm(x, norm_mix[l])
        z = h @ w_in[l]
        rq, rk, rv, rg, dq, dk, dv, cu = jnp.split(z, IN_SPLITS, axis=-1)

        ret = retention(rq.reshape(bsz, seq, RET_HEADS, RET_DK),
                        rk.reshape(bsz, seq, RET_HEADS, RET_DK) * (RET_DK ** -0.5),
                        rv.reshape(bsz, seq, RET_HEADS, RET_DV))
        ret = rms_norm(ret, ret_out_norm[l]).astype(x.dtype)
        y_a = jax.nn.silu(rg) * ret.reshape(bsz, seq, RET_WIDTH)

        qd = rms_norm(dq.reshape(bsz, seq, DIFF_HEADS, 2, DIFF_DH), diff_q_norm[l])
        kd = rms_norm(dk.reshape(bsz, seq, DIFF_HEADS, 2, DIFF_DH), diff_k_norm[l])
        lam_init = 0.8 - 0.6 * math.exp(-0.3 * l)
        lv = diff_lambda[l].astype(jnp.float32)
        lam = jnp.exp(jnp.sum(lv[0] * lv[1])) - jnp.exp(jnp.sum(lv[2] * lv[3])) + lam_init
        att = diff_attention(qd, kd, dv.reshape(bsz, seq, DIFF_HEADS, 2 * DIFF_DH), lam)
        y_b = (rms_norm(att, diff_out_norm[l]) * (1.0 - lam_init)).reshape(bsz, seq, DIFF_WIDTH)

        ca, cb = jnp.split(cu, 2, axis=-1)
        u = causal_dwconv(ca * jax.nn.sigmoid(cb), conv_w[l], conv_b[l])
        y_c = jax.nn.silu(layer_norm(u, conv_ln_g[l], conv_ln_b[l]))

        gates = jax.nn.sigmoid(h @ w_gate[l] + b_gate[l]).reshape(bsz, seq, N_BRANCH, D_MODEL)
        merged = gates[:, :, 0] * (y_a @ w_br[l, 0])
        merged = merged + gates[:, :, 1] * (y_b @ w_br[l, 1])
        merged = merged + gates[:, :, 2] * (y_c @ w_br[l, 2])
        x = x + merged @ w_o[l]

        h = rms_norm(x, norm_ffn[l])
        fg, fu = jnp.split(h @ w_ffn_in[l], 2, axis=-1)
        fg = causal_dwconv(fg, ffn_conv_w[l], ffn_conv_b[l])
        x = x + (jax.nn.gelu(fg, approximate=False) * fu) @ w_ffn_out[l]
    return x
```

```python
import functools
import math

import jax
import jax.numpy as jnp
from jax import lax
from jax.experimental import pallas as pl
from jax.experimental.pallas import tpu as pltpu

EPS = 1e-6
NEG_INF = -1e30
CHUNK = 64

RET_HEADS = 4
RET_DK = 128
RET_DV = 256
DIFF_HEADS = 4
DIFF_DH = 128
CONV_CH = 1024
CONV_K = 31
FFN_CONV_K = 3
BRANCH_WIDTH = 1024
N_BRANCH = 3

COL_RQ = 0
COL_RK = COL_RQ + RET_HEADS * RET_DK
COL_RV = COL_RK + RET_HEADS * RET_DK
COL_RG = COL_RV + RET_HEADS * RET_DV
COL_DQ = COL_RG + RET_HEADS * RET_DV
COL_DK = COL_DQ + DIFF_HEADS * 2 * DIFF_DH
COL_DV = COL_DK + DIFF_HEADS * 2 * DIFF_DH
COL_CA = COL_DV + DIFF_HEADS * 2 * DIFF_DH
COL_CB = COL_CA + CONV_CH

VMEM_LIMIT_BYTES = 56 * 1024 * 1024
CONV_HALO = 32
FFN_HALO = 8

F32 = jnp.float32
BF16 = jnp.bfloat16


def _params(*sem):
    return pltpu.CompilerParams(dimension_semantics=sem, vmem_limit_bytes=VMEM_LIMIT_BYTES)


def _tile(n, pref):
    t = min(n, pref)
    assert n % t == 0, (n, pref)
    return t


def _rmsnorm_kernel(x_ref, g_ref, o_ref):
    x = x_ref[...]
    ms = jnp.mean(x * x, axis=-1, keepdims=True)
    o_ref[...] = (x * lax.rsqrt(ms + EPS) * g_ref[...]).astype(o_ref.dtype)


def _rmsnorm(x, g):
    t, d = x.shape
    tr = _tile(t, 256)
    return pl.pallas_call(
        _rmsnorm_kernel,
        grid=(t // tr,),
        in_specs=[pl.BlockSpec((tr, d), lambda i: (i, 0)),
                  pl.BlockSpec((1, d), lambda i: (0, 0))],
        out_specs=pl.BlockSpec((tr, d), lambda i: (i, 0)),
        out_shape=jax.ShapeDtypeStruct((t, d), BF16),
        compiler_params=_params("arbitrary"),
        name="rmsnorm",
    )(x, g.reshape(1, d))


def _mm_kernel(a_ref, b_ref, o_ref):
    o_ref[...] = jnp.dot(a_ref[...], b_ref[...], preferred_element_type=F32).astype(o_ref.dtype)


def _matmul(a, b, out_dtype):
    m, k = a.shape
    _, n = b.shape
    tm, tn = _tile(m, 1024), _tile(n, 1024)
    return pl.pallas_call(
        _mm_kernel,
        grid=(m // tm, n // tn),
        in_specs=[pl.BlockSpec((tm, k), lambda i, j: (i, 0)),
                  pl.BlockSpec((k, tn), lambda i, j: (0, j))],
        out_specs=pl.BlockSpec((tm, tn), lambda i, j: (i, j)),
        out_shape=jax.ShapeDtypeStruct((m, n), out_dtype),
        compiler_params=_params("arbitrary", "arbitrary"),
        name="matmul",
    )(a, b)


def _mm_res_kernel(a_ref, b_ref, r_ref, o_ref):
    p = jnp.dot(a_ref[...], b_ref[...], preferred_element_type=F32)

    @pl.when(pl.program_id(2) == 0)
    def _():
        o_ref[...] = r_ref[...] + p

    @pl.when(pl.program_id(2) > 0)
    def _():
        o_ref[...] += p


def _matmul_residual(a, b, res):
    m, k = a.shape
    _, n = b.shape
    tm, tn, tk = _tile(m, 1024), _tile(n, 1024), _tile(k, 2048)
    return pl.pallas_call(
        _mm_res_kernel,
        grid=(m // tm, n // tn, k // tk),
        in_specs=[pl.BlockSpec((tm, tk), lambda i, j, kk: (i, kk)),
                  pl.BlockSpec((tk, tn), lambda i, j, kk: (kk, j)),
                  pl.BlockSpec((tm, tn), lambda i, j, kk: (i, j))],
        out_specs=pl.BlockSpec((tm, tn), lambda i, j, kk: (i, j)),
        out_shape=jax.ShapeDtypeStruct((m, n), F32),
        compiler_params=_params("arbitrary", "arbitrary", "arbitrary"),
        name="matmul_residual",
    )(a, b, res)


def _retention_kernel(q_ref, k_ref, v_ref, g_ref, mask_ref, qdec_ref, kdec_ref, cdec_ref, gn_ref,
                      o_ref, state_ref):
    @pl.when(pl.program_id(2) == 0)
    def _():
        state_ref[...] = jnp.zeros_like(state_ref)

    q = q_ref[...]
    k = k_ref[...]
    v = v_ref[...]
    scores = lax.dot_general(q, k, (((1,), (1,)), ((), ())), preferred_element_type=F32) * mask_ref[0]
    inner = jnp.dot(scores.astype(BF16), v, preferred_element_type=F32)
    state = state_ref[...]
    qd = (q.astype(F32) * qdec_ref[0]).astype(BF16)
    cross = jnp.dot(qd, state.astype(BF16), preferred_element_type=F32)
    kd = (k.astype(F32) * kdec_ref[0]).astype(BF16)
    kv = lax.dot_general(kd, v, (((0,), (0,)), ((), ())), preferred_element_type=F32)
    state_ref[...] = state * cdec_ref[0] + kv

    out = inner + cross
    ms = jnp.mean(out * out, axis=-1, keepdims=True)
    y = out * lax.rsqrt(ms + EPS) * gn_ref[0]
    g = g_ref[...].astype(F32)
    o_ref[...] = (g * jax.nn.sigmoid(g) * y).astype(o_ref.dtype)


def _retention_tables(blk):
    lg = jnp.log(1.0 - 2.0 ** (-5.0 - jnp.arange(RET_HEADS, dtype=F32)))
    pos = jnp.arange(blk, dtype=F32)
    dist = jnp.abs(pos[:, None] - pos[None, :])
    allowed = (jnp.arange(blk)[None, :] // CHUNK) <= (jnp.arange(blk)[:, None] // CHUNK)
    kscale = RET_DK ** -0.5
    mask = jnp.where(allowed[None], jnp.exp(dist[None] * lg[:, None, None]), 0.0) * kscale
    qdec = jnp.exp((pos[None, :] + 1.0) * lg[:, None])
    kdec = jnp.exp((blk - 1.0 - pos[None, :]) * lg[:, None]) * kscale
    cdec = jnp.exp(blk * lg)
    qdec = jnp.broadcast_to(qdec[:, :, None], (RET_HEADS, blk, RET_DK))
    kdec = jnp.broadcast_to(kdec[:, :, None], (RET_HEADS, blk, RET_DK))
    cdec = jnp.broadcast_to(cdec[:, None, None], (RET_HEADS, 1, RET_DV))
    return mask, qdec, kdec, cdec


def _retention(z, out_norm, bsz, seq):
    t = z.shape[0]
    blk = _tile(seq, 256)
    nblk = seq // blk
    mask, qdec, kdec, cdec = _retention_tables(blk)
    row = lambda b, h, i: b * nblk + i
    return pl.pallas_call(
        _retention_kernel,
        grid=(bsz, RET_HEADS, nblk),
        in_specs=[
            pl.BlockSpec((blk, RET_DK), lambda b, h, i: (row(b, h, i), COL_RQ // RET_DK + h)),
            pl.BlockSpec((blk, RET_DK), lambda b, h, i: (row(b, h, i), COL_RK // RET_DK + h)),
            pl.BlockSpec((blk, RET_DV), lambda b, h, i: (row(b, h, i), COL_RV // RET_DV + h)),
            pl.BlockSpec((blk, RET_DV), lambda b, h, i: (row(b, h, i), COL_RG // RET_DV + h)),
            pl.BlockSpec((1, blk, blk), lambda b, h, i: (h, 0, 0)),
            pl.BlockSpec((1, blk, RET_DK), lambda b, h, i: (h, 0, 0)),
            pl.BlockSpec((1, blk, RET_DK), lambda b, h, i: (h, 0, 0)),
            pl.BlockSpec((1, 1, RET_DV), lambda b, h, i: (h, 0, 0)),
            pl.BlockSpec((1, 1, RET_DV), lambda b, h, i: (h, 0, 0)),
        ],
        out_specs=pl.BlockSpec((blk, RET_DV), lambda b, h, i: (row(b, h, i), h)),
        out_shape=jax.ShapeDtypeStruct((t, RET_HEADS * RET_DV), BF16),
        scratch_shapes=[pltpu.VMEM((RET_DK, RET_DV), F32)],
        compiler_params=_params("arbitrary", "arbitrary", "arbitrary"),
        name="retention",
    )(z, z, z, z, mask, qdec, kdec, cdec, out_norm.reshape(RET_HEADS, 1, RET_DV))


def _qk_norm_kernel(q_ref, k_ref, gq_ref, gk_ref, qo_ref, ko_ref):
    scale = DIFF_DH ** -0.5
    for src, g_ref, dst, mul in ((q_ref, gq_ref, qo_ref, scale), (k_ref, gk_ref, ko_ref, 1.0)):
        g = g_ref[...] * mul
        for c in range(src.shape[1] // DIFF_DH):
            x = src[:, c * DIFF_DH:(c + 1) * DIFF_DH].astype(F32)
            ms = jnp.mean(x * x, axis=-1, keepdims=True)
            dst[:, c * DIFF_DH:(c + 1) * DIFF_DH] = (x * lax.rsqrt(ms + EPS) * g).astype(dst.dtype)


def _qk_norm(z, gq, gk):
    t = z.shape[0]
    w = DIFF_HEADS * 2 * DIFF_DH
    tr = _tile(t, 512)
    spec = lambda col: pl.BlockSpec((tr, w), lambda i: (i, col // w))
    vec = pl.BlockSpec((1, DIFF_DH), lambda i: (0, 0))
    return pl.pallas_call(
        _qk_norm_kernel,
        grid=(t // tr,),
        in_specs=[spec(COL_DQ), spec(COL_DK), vec, vec],
        out_specs=[pl.BlockSpec((tr, w), lambda i: (i, 0))] * 2,
        out_shape=[jax.ShapeDtypeStruct((t, w), BF16)] * 2,
        compiler_params=_params("arbitrary"),
        name="qk_norm",
    )(z, z, gq.reshape(1, DIFF_DH), gk.reshape(1, DIFF_DH))


def _diff_attn_kernel(q_ref, k_ref, v_ref, slope_ref, lam_ref, gn_ref, o_ref, m_sc, l_sc, acc_sc,
                      *, blk, lam_init):
    i = pl.program_id(2)
    m_sc[...] = jnp.full_like(m_sc, NEG_INF)
    l_sc[...] = jnp.zeros_like(l_sc)
    acc_sc[...] = jnp.zeros_like(acc_sc)

    q = q_ref[...]
    slope = slope_ref[0]
    r_idx = lax.broadcasted_iota(jnp.int32, (blk, blk), 0)
    c_idx = lax.broadcasted_iota(jnp.int32, (blk, blk), 1)
    rel = (r_idx - c_idx).astype(F32)

    def accumulate(j, logit_fn):
        start = pl.multiple_of(j * blk, blk)
        ks = k_ref[pl.ds(start, blk), :]
        vs = v_ref[pl.ds(start, blk), :]
        for m in range(2):
            qm = q[:, m * DIFF_DH:(m + 1) * DIFF_DH]
            km = ks[:, m * DIFF_DH:(m + 1) * DIFF_DH]
            s = lax.dot_general(qm, km, (((1,), (1,)), ((), ())), preferred_element_type=F32)
            s = logit_fn(s)
            m_old = m_sc[m]
            m_new = jnp.maximum(m_old, jnp.max(s, axis=-1, keepdims=True))
            alpha = jnp.exp(m_old - m_new)
            p = jnp.exp(s - m_new)
            l_sc[m] = alpha * l_sc[m] + jnp.sum(p, axis=-1, keepdims=True)
            acc_sc[m] = alpha * acc_sc[m] + jnp.dot(p.astype(BF16), vs, preferred_element_type=F32)
            m_sc[m] = m_new

    def past_block(j, carry):
        off = ((i - j) * blk).astype(F32)
        accumulate(j, lambda s: s - slope * (rel + off))
        return carry

    lax.fori_loop(0, i, past_block, 0)

    allowed = (c_idx // CHUNK) <= (r_idx // CHUNK)
    accumulate(i, lambda s: jnp.where(allowed, s - slope * jnp.abs(rel), NEG_INF))

    lv = lam_ref[...]
    lam = (jnp.exp(jnp.sum(lv[0:1] * lv[1:2], axis=-1, keepdims=True))
           - jnp.exp(jnp.sum(lv[2:3] * lv[3:4], axis=-1, keepdims=True)) + lam_init)
    out = acc_sc[0] / l_sc[0] - lam * (acc_sc[1] / l_sc[1])
    ms = jnp.mean(out * out, axis=-1, keepdims=True)
    o_ref[...] = (out * lax.rsqrt(ms + EPS) * gn_ref[0] * (1.0 - lam_init)).astype(o_ref.dtype)


def _diff_attention(qd, kd, z, lam_params, out_norm, lam_init, bsz, seq):
    t = qd.shape[0]
    hw = 2 * DIFF_DH
    blk = _tile(seq, 512)
    nblk = seq // blk
    slopes = 2.0 ** (-8.0 * (jnp.arange(DIFF_HEADS, dtype=F32) + 1.0) / DIFF_HEADS)
    slopes = jnp.broadcast_to(slopes[:, None, None], (DIFF_HEADS, 1, blk))
    kernel = functools.partial(_diff_attn_kernel, blk=blk, lam_init=lam_init)
    return pl.pallas_call(
        kernel,
        grid=(bsz, DIFF_HEADS, nblk),
        in_specs=[
            pl.BlockSpec((blk, hw), lambda b, h, i: (b * nblk + i, h)),
            pl.BlockSpec((seq, hw), lambda b, h, i: (b, h)),
            pl.BlockSpec((seq, hw), lambda b, h, i: (b, COL_DV // hw + h)),
            pl.BlockSpec((1, 1, blk), lambda b, h, i: (h, 0, 0)),
            pl.BlockSpec((4, DIFF_DH), lambda b, h, i: (0, 0)),
            pl.BlockSpec((1, 1, hw), lambda b, h, i: (h, 0, 0)),
        ],
        out_specs=pl.BlockSpec((blk, hw), lambda b, h, i: (b * nblk + i, h)),
        out_shape=jax.ShapeDtypeStruct((t, DIFF_HEADS * hw), BF16),
        scratch_shapes=[pltpu.VMEM((2, blk, 1), F32), pltpu.VMEM((2, blk, 1), F32),
                        pltpu.VMEM((2, blk, hw), F32)],
        compiler_params=_params("arbitrary", "arbitrary", "arbitrary"),
        name="diff_attention",
    )(qd, kd, z, slopes, lam_params, out_norm.reshape(DIFF_HEADS, 1, hw))


def _conv_module_kernel(ca_ref, cb_ref, cah_ref, cbh_ref, w_ref, b_ref, g_ref, beta_ref, o_ref, xe_ref,
                        *, rows, sub):
    first = pl.program_id(1) == 0
    halo = cah_ref[...].astype(F32) * jax.nn.sigmoid(cbh_ref[...].astype(F32))
    xe_ref[0:CONV_HALO, :] = jnp.where(first, 0.0, halo)
    xe_ref[CONV_HALO:CONV_HALO + rows, :] = ca_ref[...].astype(F32) * jax.nn.sigmoid(cb_ref[...].astype(F32))

    base = CONV_HALO - (CONV_K - 1)
    for r0 in range(0, rows, sub):
        acc = jnp.broadcast_to(b_ref[...], (sub, CONV_CH))
        for k in range(CONV_K):
            acc = acc + w_ref[k:k + 1, :] * xe_ref[base + r0 + k:base + r0 + k + sub, :]
        mu = jnp.mean(acc, axis=-1, keepdims=True)
        xc = acc - mu
        var = jnp.mean(xc * xc, axis=-1, keepdims=True)
        y = xc * lax.rsqrt(var + EPS) * g_ref[...] + beta_ref[...]
        o_ref[r0:r0 + sub, :] = (y * jax.nn.sigmoid(y)).astype(o_ref.dtype)


def _conv_module(z, w, b, ln_g, ln_b, bsz, seq):
    t = z.shape[0]
    rows = _tile(seq, 256)
    nblk = seq // rows
    hpb = rows // CONV_HALO
    main = lambda col: pl.BlockSpec((rows, CONV_CH), lambda bb, i: (bb * nblk + i, col // CONV_CH))
    halo = lambda col: pl.BlockSpec(
        (CONV_HALO, CONV_CH), lambda bb, i: (jnp.maximum((bb * nblk + i) * hpb - 1, 0), col // CONV_CH))
    vec = pl.BlockSpec((1, CONV_CH), lambda bb, i: (0, 0))
    kernel = functools.partial(_conv_module_kernel, rows=rows, sub=32)
    return pl.pallas_call(
        kernel,
        grid=(bsz, nblk),
        in_specs=[main(COL_CA), main(COL_CB), halo(COL_CA), halo(COL_CB),
                  pl.BlockSpec((CONV_K, CONV_CH), lambda bb, i: (0, 0)), vec, vec, vec],
        out_specs=pl.BlockSpec((rows, CONV_CH), lambda bb, i: (bb * nblk + i, 0)),
        out_shape=jax.ShapeDtypeStruct((t, CONV_CH), BF16),
        scratch_shapes=[pltpu.VMEM((CONV_HALO + rows, CONV_CH), F32)],
        compiler_params=_params("arbitrary", "arbitrary"),
        name="conv_module",
    )(z, z, z, z, w, b.reshape(1, CONV_CH), ln_g.reshape(1, CONV_CH), ln_b.reshape(1, CONV_CH))


def _merge_kernel(h_ref, wg_ref, bg_ref, ya_ref, yb_ref, yc_ref, wbr_ref, o_ref):
    h = h_ref[...]
    merged = None
    for j, y_ref in enumerate((ya_ref, yb_ref, yc_ref)):
        gate = jax.nn.sigmoid(jnp.dot(h, wg_ref[j], preferred_element_type=F32) + bg_ref[j])
        term = gate * jnp.dot(y_ref[...], wbr_ref[j], preferred_element_type=F32)
        merged = term if merged is None else merged + term
    o_ref[...] = merged.astype(o_ref.dtype)


def _gated_merge(h, w_gate, b_gate, ya, yb, yc, w_br):
    t, d = h.shape
    tm, tn = _tile(t, 512), _tile(d, 512)
    ysp = pl.BlockSpec((tm, BRANCH_WIDTH), lambda n, m: (m, 0))
    return pl.pallas_call(
        _merge_kernel,
        grid=(d // tn, t // tm),
        in_specs=[pl.BlockSpec((tm, d), lambda n, m: (m, 0)),
                  pl.BlockSpec((N_BRANCH, d, tn), lambda n, m: (0, 0, n)),
                  pl.BlockSpec((N_BRANCH, 1, tn), lambda n, m: (0, 0, n)),
                  ysp, ysp, ysp,
                  pl.BlockSpec((N_BRANCH, BRANCH_WIDTH, tn), lambda n, m: (0, 0, n))],
        out_specs=pl.BlockSpec((tm, tn), lambda n, m: (m, n)),
        out_shape=jax.ShapeDtypeStruct((t, d), BF16),
        compiler_params=_params("arbitrary", "arbitrary"),
        name="gated_merge",
    )(h, w_gate, b_gate, ya, yb, yc, w_br)


def _ffn_in_kernel(h_ref, wg_ref, wu_ref, cw_ref, cb_ref, o_ref, fg_ref, *, tm, tiles_per_seq):
    h = h_ref[...]
    fg = jnp.dot(h, wg_ref[...], preferred_element_type=F32)
    fu = jnp.dot(h, wu_ref[...], preferred_element_type=F32)

    seq_start = pl.program_id(1) % tiles_per_seq == 0

    @pl.when(seq_start)
    def _():
        fg_ref[0:FFN_HALO, :] = jnp.zeros((FFN_HALO, fg_ref.shape[1]), F32)

    @pl.when(jnp.logical_not(seq_start))
    def _():
        fg_ref[0:FFN_HALO, :] = fg_ref[tm:tm + FFN_HALO, :]

    fg_ref[FFN_HALO:FFN_HALO + tm, :] = fg

    conv = (cw_ref[0:1, :] * fg_ref[FFN_HALO - 2:FFN_HALO - 2 + tm, :]
            + cw_ref[1:2, :] * fg_ref[FFN_HALO - 1:FFN_HALO - 1 + tm, :]
            + cw_ref[2:3, :] * fg + cb_ref[...])
    gelu = 0.5 * conv * (1.0 + lax.erf(conv * (2.0 ** -0.5)))
    o_ref[...] = (gelu * fu).astype(o_ref.dtype)


def _ffn_in(h, w_in, conv_w, conv_b, seq):
    t, d = h.shape
    f = w_in.shape[1] // 2
    tm, tn = _tile(seq, 512), _tile(f, 512)
    nf = f // tn
    kernel = functools.partial(_ffn_in_kernel, tm=tm, tiles_per_seq=seq // tm)
    return pl.pallas_call(
        kernel,
        grid=(nf, t // tm),
        in_specs=[pl.BlockSpec((tm, d), lambda n, m: (m, 0)),
                  pl.BlockSpec((d, tn), lambda n, m: (0, n)),
                  pl.BlockSpec((d, tn), lambda n, m: (0, nf + n)),
                  pl.BlockSpec((FFN_CONV_K, tn), lambda n, m: (0, n)),
                  pl.BlockSpec((1, tn), lambda n, m: (0, n))],
        out_specs=pl.BlockSpec((tm, tn), lambda n, m: (m, n)),
        out_shape=jax.ShapeDtypeStruct((t, f), BF16),
        scratch_shapes=[pltpu.VMEM((FFN_HALO + tm, tn), F32)],
        compiler_params=_params("arbitrary", "arbitrary"),
        name="ffn_in",
    )(h, w_in, w_in, conv_w, conv_b.reshape(1, f))


def kernel(x, norm_mix, w_in, w_gate, b_gate, diff_q_norm, diff_k_norm, diff_lambda, diff_out_norm,
           ret_out_norm, conv_w, conv_b, conv_ln_g, conv_ln_b, w_br, w_o, norm_ffn, w_ffn_in,
           ffn_conv_w, ffn_conv_b, w_ffn_out):
    bsz, seq, d = x.shape
    depth = w_in.shape[0]
    t = bsz * seq
    x = x.reshape(t, d)
    for l in range(depth):
        lam_init = 0.8 - 0.6 * math.exp(-0.3 * l)
        h = _rmsnorm(x, norm_mix[l])
        z = _matmul(h, w_in[l].astype(BF16), BF16)
        ya = _retention(z, ret_out_norm[l], bsz, seq)
        qd, kd = _qk_norm(z, diff_q_norm[l], diff_k_norm[l])
        yb = _diff_attention(qd, kd, z, diff_lambda[l], diff_out_norm[l], lam_init, bsz, seq)
        yc = _conv_module(z, conv_w[l], conv_b[l], conv_ln_g[l], conv_ln_b[l], bsz, seq)
        merged = _gated_merge(h, w_gate[l].astype(BF16).reshape(d, N_BRANCH, d).transpose(1, 0, 2),
                              b_gate[l].reshape(N_BRANCH, 1, d), ya, yb, yc, w_br[l].astype(BF16))
        x = _matmul_residual(merged, w_o[l].astype(BF16), x)
        h = _rmsnorm(x, norm_ffn[l])
        act = _ffn_in(h, w_ffn_in[l].astype(BF16), ffn_conv_w[l], ffn_conv_b[l], seq)
        x = _matmul_residual(act, w_ffn_out[l].astype(BF16), x)
    return x.reshape(bsz, seq, d)
```

```python
import functools
import math

import jax
import jax.numpy as jnp
from jax import lax
from jax.experimental import pallas as pl
from jax.experimental.pallas import tpu as pltpu

EPS = 1e-6
NEG_INF = -1e30
CHUNK = 64
LOG2E = 1.4426950408889634

RET_HEADS = 4
RET_DK = 128
RET_DV = 256
DIFF_HEADS = 4
DIFF_DH = 128
CONV_CH = 1024
CONV_K = 31
FFN_CONV_K = 3
BRANCH_WIDTH = 1024
N_BRANCH = 3

COL_RQ = 0
COL_RK = COL_RQ + RET_HEADS * RET_DK
COL_RV = COL_RK + RET_HEADS * RET_DK
COL_RG = COL_RV + RET_HEADS * RET_DV
COL_DQ = COL_RG + RET_HEADS * RET_DV
COL_DK = COL_DQ + DIFF_HEADS * 2 * DIFF_DH
COL_DV = COL_DK + DIFF_HEADS * 2 * DIFF_DH
COL_CA = COL_DV + DIFF_HEADS * 2 * DIFF_DH
COL_CB = COL_CA + CONV_CH

VMEM_LIMIT_BYTES = 56 * 1024 * 1024
CONV_HALO = 32
FFN_HALO = 8

F32 = jnp.float32
BF16 = jnp.bfloat16


def _params(*sem):
    return pltpu.CompilerParams(dimension_semantics=sem, vmem_limit_bytes=VMEM_LIMIT_BYTES)


def _tile(n, pref):
    t = min(n, pref)
    assert n % t == 0, (n, pref)
    return t


def _rmsnorm_kernel(x_ref, g_ref, o_ref):
    x = x_ref[...]
    ms = jnp.mean(x * x, axis=-1, keepdims=True)
    o_ref[...] = (x * lax.rsqrt(ms + EPS) * g_ref[...]).astype(o_ref.dtype)


def _rmsnorm(x, g):
    t, d = x.shape
    tr = _tile(t, 256)
    return pl.pallas_call(
        _rmsnorm_kernel,
        grid=(t // tr,),
        in_specs=[pl.BlockSpec((tr, d), lambda i: (i, 0)),
                  pl.BlockSpec((1, d), lambda i: (0, 0))],
        out_specs=pl.BlockSpec((tr, d), lambda i: (i, 0)),
        out_shape=jax.ShapeDtypeStruct((t, d), BF16),
        compiler_params=_params("arbitrary"),
        name="rmsnorm",
    )(x, g.reshape(1, d))


def _mm_kernel(a_ref, b_ref, o_ref):
    o_ref[...] = jnp.dot(a_ref[...], b_ref[...], preferred_element_type=F32).astype(o_ref.dtype)


def _matmul(a, b, out_dtype):
    m, k = a.shape
    _, n = b.shape
    tm, tn = _tile(m, 1024), _tile(n, 1024)
    return pl.pallas_call(
        _mm_kernel,
        grid=(m // tm, n // tn),
        in_specs=[pl.BlockSpec((tm, k), lambda i, j: (i, 0)),
                  pl.BlockSpec((k, tn), lambda i, j: (0, j))],
        out_specs=pl.BlockSpec((tm, tn), lambda i, j: (i, j)),
        out_shape=jax.ShapeDtypeStruct((m, n), out_dtype),
        compiler_params=_params("arbitrary", "arbitrary"),
        name="matmul",
    )(a, b)


def _mm_res_kernel(a_ref, b_ref, r_ref, o_ref):
    o_ref[...] = r_ref[...] + jnp.dot(a_ref[...], b_ref[...], preferred_element_type=F32)


def _matmul_residual(a, b, res):
    m, k = a.shape
    _, n = b.shape
    tm = _tile(m, 512)
    tn = _tile(n, (4 * 1024 * 1024) // k)
    return pl.pallas_call(
        _mm_res_kernel,
        grid=(m // tm, n // tn),
        in_specs=[pl.BlockSpec((tm, k), lambda i, j: (i, 0)),
                  pl.BlockSpec((k, tn), lambda i, j: (0, j)),
                  pl.BlockSpec((tm, tn), lambda i, j: (i, j))],
        out_specs=pl.BlockSpec((tm, tn), lambda i, j: (i, j)),
        out_shape=jax.ShapeDtypeStruct((m, n), F32),
        compiler_params=_params("arbitrary", "arbitrary"),
        name="matmul_residual",
    )(a, b, res)


def _retention_kernel(q_ref, k_ref, v_ref, g_ref, mask_ref, qdec_ref, kdec_ref, cdec_ref, gn_ref,
                      o_ref, state_ref):
    @pl.when(pl.program_id(2) == 0)
    def _():
        state_ref[...] = jnp.zeros_like(state_ref)

    q = q_ref[...]
    k = k_ref[...]
    v = v_ref[...]
    scores = lax.dot_general(q, k, (((1,), (1,)), ((), ())), preferred_element_type=F32) * mask_ref[0]
    inner = jnp.dot(scores.astype(BF16), v, preferred_element_type=F32)
    state = state_ref[...]
    qd = (q.astype(F32) * qdec_ref[0]).astype(BF16)
    cross = jnp.dot(qd, state.astype(BF16), preferred_element_type=F32)
    kd = (k.astype(F32) * kdec_ref[0]).astype(BF16)
    kv = lax.dot_general(kd, v, (((0,), (0,)), ((), ())), preferred_element_type=F32)
    state_ref[...] = state * cdec_ref[0] + kv

    out = inner + cross
    ms = jnp.mean(out * out, axis=-1, keepdims=True)
    y = out * lax.rsqrt(ms + EPS) * gn_ref[0]
    g = g_ref[...].astype(F32)
    o_ref[...] = (g * jax.nn.sigmoid(g) * y).astype(o_ref.dtype)


def _retention_tables(blk):
    lg = jnp.log(1.0 - 2.0 ** (-5.0 - jnp.arange(RET_HEADS, dtype=F32)))
    pos = jnp.arange(blk, dtype=F32)
    dist = jnp.abs(pos[:, None] - pos[None, :])
    allowed = (jnp.arange(blk)[None, :] // CHUNK) <= (jnp.arange(blk)[:, None] // CHUNK)
    kscale = RET_DK ** -0.5
    mask = jnp.where(allowed[None], jnp.exp(dist[None] * lg[:, None, None]), 0.0) * kscale
    qdec = jnp.exp((pos[None, :] + 1.0) * lg[:, None])
    kdec = jnp.exp((blk - 1.0 - pos[None, :]) * lg[:, None]) * kscale
    cdec = jnp.exp(blk * lg)
    qdec = jnp.broadcast_to(qdec[:, :, None], (RET_HEADS, blk, RET_DK))
    kdec = jnp.broadcast_to(kdec[:, :, None], (RET_HEADS, blk, RET_DK))
    cdec = jnp.broadcast_to(cdec[:, None, None], (RET_HEADS, 1, RET_DV))
    return mask, qdec, kdec, cdec


def _retention(z, out_norm, bsz, seq):
    t = z.shape[0]
    blk = _tile(seq, 256)
    nblk = seq // blk
    mask, qdec, kdec, cdec = _retention_tables(blk)
    row = lambda b, h, i: b * nblk + i
    return pl.pallas_call(
        _retention_kernel,
        grid=(bsz, RET_HEADS, nblk),
        in_specs=[
            pl.BlockSpec((blk, RET_DK), lambda b, h, i: (row(b, h, i), COL_RQ // RET_DK + h)),
            pl.BlockSpec((blk, RET_DK), lambda b, h, i: (row(b, h, i), COL_RK // RET_DK + h)),
            pl.BlockSpec((blk, RET_DV), lambda b, h, i: (row(b, h, i), COL_RV // RET_DV + h)),
            pl.BlockSpec((blk, RET_DV), lambda b, h, i: (row(b, h, i), COL_RG // RET_DV + h)),
            pl.BlockSpec((1, blk, blk), lambda b, h, i: (h, 0, 0)),
            pl.BlockSpec((1, blk, RET_DK), lambda b, h, i: (h, 0, 0)),
            pl.BlockSpec((1, blk, RET_DK), lambda b, h, i: (h, 0, 0)),
            pl.BlockSpec((1, 1, RET_DV), lambda b, h, i: (h, 0, 0)),
            pl.BlockSpec((1, 1, RET_DV), lambda b, h, i: (h, 0, 0)),
        ],
        out_specs=pl.BlockSpec((blk, RET_DV), lambda b, h, i: (row(b, h, i), h)),
        out_shape=jax.ShapeDtypeStruct((t, RET_HEADS * RET_DV), BF16),
        scratch_shapes=[pltpu.VMEM((RET_DK, RET_DV), F32)],
        compiler_params=_params("arbitrary", "arbitrary", "arbitrary"),
        name="retention",
    )(z, z, z, z, mask, qdec, kdec, cdec, out_norm.reshape(RET_HEADS, 1, RET_DV))


def _qk_norm_kernel(q_ref, k_ref, gq_ref, gk_ref, qo_ref, ko_ref):
    scale = DIFF_DH ** -0.5 * LOG2E
    for src, g_ref, dst, mul in ((q_ref, gq_ref, qo_ref, scale), (k_ref, gk_ref, ko_ref, 1.0)):
        g = g_ref[...] * mul
        for c in range(src.shape[1] // DIFF_DH):
            x = src[:, c * DIFF_DH:(c + 1) * DIFF_DH].astype(F32)
            ms = jnp.mean(x * x, axis=-1, keepdims=True)
            dst[:, c * DIFF_DH:(c + 1) * DIFF_DH] = (x * lax.rsqrt(ms + EPS) * g).astype(dst.dtype)


def _qk_norm(z, gq, gk):
    t = z.shape[0]
    w = DIFF_HEADS * 2 * DIFF_DH
    tr = _tile(t, 512)
    spec = lambda col: pl.BlockSpec((tr, w), lambda i: (i, col // w))
    vec = pl.BlockSpec((1, DIFF_DH), lambda i: (0, 0))
    return pl.pallas_call(
        _qk_norm_kernel,
        grid=(t // tr,),
        in_specs=[spec(COL_DQ), spec(COL_DK), vec, vec],
        out_specs=[pl.BlockSpec((tr, w), lambda i: (i, 0))] * 2,
        out_shape=[jax.ShapeDtypeStruct((t, w), BF16)] * 2,
        compiler_params=_params("arbitrary"),
        name="qk_norm",
    )(z, z, gq.reshape(1, DIFF_DH), gk.reshape(1, DIFF_DH))


def _diff_attn_kernel(qt_ref, k_ref, vt_ref, nb_ref, slope_ref, lam_ref, gn_ref, o_ref,
                      m_sc, l_sc, acc_sc, *, blk, lam_init):
    i = pl.program_id(2)
    m_sc[...] = jnp.full_like(m_sc, NEG_INF)
    l_sc[...] = jnp.zeros_like(l_sc)
    acc_sc[...] = jnp.zeros_like(acc_sc)

    def accumulate(j, carry):
        start = pl.multiple_of(j * blk, blk)
        ks = k_ref[pl.ds(start, blk), :]
        vts = vt_ref[:, pl.ds(start, blk)]
        bias = nb_ref[0, (j == i).astype(jnp.int32)]
        soff = slope_ref[0] * ((i - j) * blk).astype(F32)
        s1s = []
        for m in range(2):
            st = jnp.dot(ks[:, m * DIFF_DH:(m + 1) * DIFF_DH], qt_ref[m * DIFF_DH:(m + 1) * DIFF_DH, :],
                         preferred_element_type=F32)
            s1s.append(st + bias)
        ps, alphas = [], []
        for m in range(2):
            m_old = m_sc[m]
            m_new = jnp.maximum(m_old, jnp.max(s1s[m], axis=0, keepdims=True) - soff)
            alpha = jnp.exp2(m_old - m_new)
            p = jnp.exp2(s1s[m] - (m_new + soff))
            l_sc[m] = alpha * l_sc[m] + jnp.sum(p, axis=0, keepdims=True)
            m_sc[m] = m_new
            ps.append(p.astype(BF16))
            alphas.append(alpha)
        for m in range(2):
            acc_sc[m] = alphas[m] * acc_sc[m] + jnp.dot(vts, ps[m], preferred_element_type=F32)
        return carry

    lax.fori_loop(0, i + 1, accumulate, 0)

    lv = lam_ref[...]
    lam = (jnp.exp(jnp.sum(lv[0:1] * lv[1:2], axis=-1, keepdims=True))
           - jnp.exp(jnp.sum(lv[2:3] * lv[3:4], axis=-1, keepdims=True)) + lam_init)
    out = acc_sc[0] / l_sc[0] - lam * (acc_sc[1] / l_sc[1])
    ms = jnp.mean(out * out, axis=0, keepdims=True)
    o_ref[...] = (out * lax.rsqrt(ms + EPS) * gn_ref[0] * (1.0 - lam_init)).astype(o_ref.dtype)


def _diff_attention(qt, kd, vt, lam_params, out_norm, lam_init, bsz, seq):
    t = kd.shape[0]
    hw = 2 * DIFF_DH
    blk = _tile(seq, 512)
    nblk = seq // blk
    slopes = 2.0 ** (-8.0 * (jnp.arange(DIFF_HEADS, dtype=F32) + 1.0) / DIFF_HEADS) * LOG2E
    r = jnp.arange(blk)[None, :]
    c = jnp.arange(blk)[:, None]
    past = -slopes[:, None, None] * (r - c).astype(F32)[None]
    allowed = (c // CHUNK) <= (r // CHUNK)
    diag = jnp.where(allowed[None], -slopes[:, None, None] * jnp.abs(r - c).astype(F32)[None], NEG_INF)
    nb = jnp.stack([past, diag], axis=1)
    slope_rows = jnp.broadcast_to(slopes[:, None, None], (DIFF_HEADS, 1, blk))
    gn = jnp.broadcast_to(out_norm.reshape(DIFF_HEADS, hw, 1), (DIFF_HEADS, hw, blk))
    kernel = functools.partial(_diff_attn_kernel, blk=blk, lam_init=lam_init)
    return pl.pallas_call(
        kernel,
        grid=(bsz, DIFF_HEADS, nblk),
        in_specs=[
            pl.BlockSpec((hw, blk), lambda b, h, i: (h, b * nblk + i)),
            pl.BlockSpec((seq, hw), lambda b, h, i: (b, h)),
            pl.BlockSpec((hw, seq), lambda b, h, i: (h, b)),
            pl.BlockSpec((1, 2, blk, blk), lambda b, h, i: (h, 0, 0, 0)),
            pl.BlockSpec((1, 1, blk), lambda b, h, i: (h, 0, 0)),
            pl.BlockSpec((4, DIFF_DH), lambda b, h, i: (0, 0)),
            pl.BlockSpec((1, hw, blk), lambda b, h, i: (h, 0, 0)),
        ],
        out_specs=pl.BlockSpec((hw, blk), lambda b, h, i: (h, b * nblk + i)),
        out_shape=jax.ShapeDtypeStruct((DIFF_HEADS * hw, t), BF16),
        scratch_shapes=[pltpu.VMEM((2, 1, blk), F32), pltpu.VMEM((2, 1, blk), F32),
                        pltpu.VMEM((2, hw, blk), F32)],
        compiler_params=_params("arbitrary", "arbitrary", "arbitrary"),
        name="diff_attention",
    )(qt, kd, vt, nb, slope_rows, lam_params, gn)


def _conv_module_kernel(ca_ref, cb_ref, cah_ref, cbh_ref, w_ref, b_ref, g_ref, beta_ref, o_ref, xe_ref,
                        xs_ref, *, rows, sub):
    first = pl.program_id(1) == 0
    halo = cah_ref[...].astype(F32) * jax.nn.sigmoid(cbh_ref[...].astype(F32))
    xe_ref[0:CONV_HALO, :] = jnp.where(first, 0.0, halo)
    xe_ref[CONV_HALO:CONV_HALO + rows, :] = ca_ref[...].astype(F32) * jax.nn.sigmoid(cb_ref[...].astype(F32))

    base = CONV_HALO - (CONV_K - 1)
    span = xs_ref.shape[1]
    for s in range(1, 8):
        xs_ref[s - 1] = xe_ref[s:s + span, :]

    for r0 in range(0, rows, sub):
        acc = jnp.broadcast_to(b_ref[...], (sub, CONV_CH))
        for k in range(CONV_K):
            a, s = divmod(base + k, 8)
            src = xe_ref if s == 0 else xs_ref.at[s - 1]
            acc = acc + pltpu.repeat(w_ref[k], sub // 8, axis=0) * src[r0 + 8 * a:r0 + 8 * a + sub, :]
        mu = jnp.mean(acc, axis=-1, keepdims=True)
        xc = acc - mu
        var = jnp.mean(xc * xc, axis=-1, keepdims=True)
        y = xc * lax.rsqrt(var + EPS) * g_ref[...] + beta_ref[...]
        o_ref[r0:r0 + sub, :] = (y * jax.nn.sigmoid(y)).astype(o_ref.dtype)


def _conv_module(z, w, b, ln_g, ln_b, bsz, seq):
    t = z.shape[0]
    rows = _tile(seq, 256)
    nblk = seq // rows
    hpb = rows // CONV_HALO
    main = lambda col: pl.BlockSpec((rows, CONV_CH), lambda bb, i: (bb * nblk + i, col // CONV_CH))
    halo = lambda col: pl.BlockSpec(
        (CONV_HALO, CONV_CH), lambda bb, i: (jnp.maximum((bb * nblk + i) * hpb - 1, 0), col // CONV_CH))
    vec = pl.BlockSpec((1, CONV_CH), lambda bb, i: (0, 0))
    kernel = functools.partial(_conv_module_kernel, rows=rows, sub=32)
    return pl.pallas_call(
        kernel,
        grid=(bsz, nblk),
        in_specs=[main(COL_CA), main(COL_CB), halo(COL_CA), halo(COL_CB),
                  pl.BlockSpec((CONV_K, 8, CONV_CH), lambda bb, i: (0, 0, 0)), vec, vec, vec],
        out_specs=pl.BlockSpec((rows, CONV_CH), lambda bb, i: (bb * nblk + i, 0)),
        out_shape=jax.ShapeDtypeStruct((t, CONV_CH), BF16),
        scratch_shapes=[pltpu.VMEM((CONV_HALO + rows, CONV_CH), F32),
                        pltpu.VMEM((7, CONV_HALO + rows - 8, CONV_CH), F32)],
        compiler_params=_params("arbitrary", "arbitrary"),
        name="conv_module",
    )(z, z, z, z, jnp.broadcast_to(w[:, None, :], (CONV_K, 8, CONV_CH)), b.reshape(1, CONV_CH),
      ln_g.reshape(1, CONV_CH), ln_b.reshape(1, CONV_CH))


def _merge_kernel(h_ref, wg0_ref, wg1_ref, wg2_ref, bg0_ref, bg1_ref, bg2_ref, ya_ref, yb_ref, yc_ref,
                  wbr_ref, o_ref):
    h = h_ref[...]
    merged = None
    for j, (wg_ref, bg_ref, y_ref) in enumerate(((wg0_ref, bg0_ref, ya_ref), (wg1_ref, bg1_ref, yb_ref),
                                                 (wg2_ref, bg2_ref, yc_ref))):
        gate = jax.nn.sigmoid(jnp.dot(h, wg_ref[...], preferred_element_type=F32) + bg_ref[...])
        term = gate * jnp.dot(y_ref[...], wbr_ref[j], preferred_element_type=F32)
        merged = term if merged is None else merged + term
    o_ref[...] = merged.astype(o_ref.dtype)


def _gated_merge(h, w_gate, b_gate, ya, yb, yc, w_br):
    t, d = h.shape
    tm, tn = _tile(t, 512), _tile(d, 512)
    nd = d // tn
    ysp = pl.BlockSpec((tm, BRANCH_WIDTH), lambda n, m: (m, 0))
    wsp = [pl.BlockSpec((d, tn), lambda n, m, j=j: (0, j * nd + n)) for j in range(N_BRANCH)]
    bsp = [pl.BlockSpec((1, tn), lambda n, m, j=j: (0, j * nd + n)) for j in range(N_BRANCH)]
    return pl.pallas_call(
        _merge_kernel,
        grid=(nd, t // tm),
        in_specs=[pl.BlockSpec((tm, d), lambda n, m: (m, 0)), *wsp, *bsp, ysp, ysp, ysp,
                  pl.BlockSpec((N_BRANCH, BRANCH_WIDTH, tn), lambda n, m: (0, 0, n))],
        out_specs=pl.BlockSpec((tm, tn), lambda n, m: (m, n)),
        out_shape=jax.ShapeDtypeStruct((t, d), BF16),
        compiler_params=_params("arbitrary", "arbitrary"),
        name="gated_merge",
    )(h, w_gate, w_gate, w_gate, b_gate, b_gate, b_gate, ya, yb, yc, w_br)


def _ffn_in_kernel(h_ref, wg_ref, wu_ref, cw_ref, cb_ref, o_ref, fg_ref, *, tm, tn, rsub, csub,
                   tiles_per_seq):
    seq_start = pl.program_id(1) % tiles_per_seq == 0

    @pl.when(seq_start)
    def _():
        fg_ref[0:FFN_HALO, :] = jnp.zeros((FFN_HALO, tn), F32)

    @pl.when(jnp.logical_not(seq_start))
    def _():
        fg_ref[0:FFN_HALO, :] = fg_ref[tm:tm + FFN_HALO, :]

    def project(r0, c0):
        h = h_ref[r0:r0 + rsub, :]
        fg = jnp.dot(h, wg_ref[:, c0:c0 + csub], preferred_element_type=F32)
        fu = jnp.dot(h, wu_ref[:, c0:c0 + csub], preferred_element_type=F32)
        fg_ref[FFN_HALO + r0:FFN_HALO + r0 + rsub, c0:c0 + csub] = fg
        return fg, fu

    def finish(r0, c0, fg, fu):
        cols = slice(c0, c0 + csub)
        conv = (cw_ref[0:1, cols] * fg_ref[FFN_HALO - 2 + r0:FFN_HALO - 2 + r0 + rsub, cols]
                + cw_ref[1:2, cols] * fg_ref[FFN_HALO - 1 + r0:FFN_HALO - 1 + r0 + rsub, cols]
                + cw_ref[2:3, cols] * fg + cb_ref[:, cols])
        gelu = 0.5 * conv * (1.0 + lax.erf(conv * (2.0 ** -0.5)))
        o_ref[r0:r0 + rsub, cols] = (gelu * fu).astype(o_ref.dtype)

    pending = None
    for r0 in range(0, tm, rsub):
        for c0 in range(0, tn, csub):
            projected = project(r0, c0)
            if pending is not None:
                finish(*pending)
            pending = (r0, c0, *projected)
    finish(*pending)


def _ffn_in(h, w_in, conv_w, conv_b, seq):
    t, d = h.shape
    f = w_in.shape[1] // 2
    tm, tn = _tile(seq, 1024), _tile(f, 512)
    rsub, csub = _tile(tm, 512), _tile(tn, 256)
    nf = f // tn
    kernel = functools.partial(_ffn_in_kernel, tm=tm, tn=tn, rsub=rsub, csub=csub, tiles_per_seq=seq // tm)
    return pl.pallas_call(
        kernel,
        grid=(nf, t // tm),
        in_specs=[pl.BlockSpec((tm, d), lambda n, m: (m, 0)),
                  pl.BlockSpec((d, tn), lambda n, m: (0, n)),
                  pl.BlockSpec((d, tn), lambda n, m: (0, nf + n)),
                  pl.BlockSpec((FFN_CONV_K, tn), lambda n, m: (0, n)),
                  pl.BlockSpec((1, tn), lambda n, m: (0, n))],
        out_specs=pl.BlockSpec((tm, tn), lambda n, m: (m, n)),
        out_shape=jax.ShapeDtypeStruct((t, f), BF16),
        scratch_shapes=[pltpu.VMEM((FFN_HALO + tm, tn), F32)],
        compiler_params=_params("arbitrary", "arbitrary"),
        name="ffn_in",
    )(h, w_in, w_in, conv_w, conv_b.reshape(1, f))


def kernel(x, norm_mix, w_in, w_gate, b_gate, diff_q_norm, diff_k_norm, diff_lambda, diff_out_norm,
           ret_out_norm, conv_w, conv_b, conv_ln_g, conv_ln_b, w_br, w_o, norm_ffn, w_ffn_in,
           ffn_conv_w, ffn_conv_b, w_ffn_out):
    bsz, seq, d = x.shape
    depth = w_in.shape[0]
    t = bsz * seq
    x = x.reshape(t, d)
    for l in range(depth):
        lam_init = 0.8 - 0.6 * math.exp(-0.3 * l)
        h = _rmsnorm(x, norm_mix[l])
        z = _matmul(h, w_in[l].astype(BF16), BF16)
        ya = _retention(z, ret_out_norm[l], bsz, seq)
        qd, kd = _qk_norm(z, diff_q_norm[l], diff_k_norm[l])
        vt = z[:, COL_DV:COL_DV + DIFF_HEADS * 2 * DIFF_DH].T
        yb = _diff_attention(qd.T, kd, vt, diff_lambda[l], diff_out_norm[l], lam_init, bsz, seq).T
        yc = _conv_module(z, conv_w[l], conv_b[l], conv_ln_g[l], conv_ln_b[l], bsz, seq)
        merged = _gated_merge(h, w_gate[l].astype(BF16), b_gate[l].reshape(1, N_BRANCH * d), ya, yb, yc,
                              w_br[l].astype(BF16))
        x = _matmul_residual(merged, w_o[l].astype(BF16), x)
        h = _rmsnorm(x, norm_ffn[l])
        act = _ffn_in(h, w_ffn_in[l].astype(BF16), ffn_conv_w[l], ffn_conv_b[l], seq)
        x = _matmul_residual(act, w_ffn_out[l].astype(BF16), x)
    return x.reshape(bsz, seq, d)
```

```python
import functools
import math

import jax
import jax.numpy as jnp
from jax import lax
from jax.experimental import pallas as pl
from jax.experimental.pallas import tpu as pltpu

EPS = 1e-6
NEG_INF = -1e30
CHUNK = 64
LOG2E = 1.4426950408889634

RET_HEADS = 4
RET_DK = 128
RET_DV = 256
DIFF_HEADS = 4
DIFF_DH = 128
CONV_CH = 1024
CONV_K = 31
FFN_CONV_K = 3
BRANCH_WIDTH = 1024
N_BRANCH = 3

COL_RQ = 0
COL_RK = COL_RQ + RET_HEADS * RET_DK
COL_RV = COL_RK + RET_HEADS * RET_DK
COL_RG = COL_RV + RET_HEADS * RET_DV
COL_DQ = COL_RG + RET_HEADS * RET_DV
COL_DK = COL_DQ + DIFF_HEADS * 2 * DIFF_DH
COL_DV = COL_DK + DIFF_HEADS * 2 * DIFF_DH
COL_CA = COL_DV + DIFF_HEADS * 2 * DIFF_DH
COL_CB = COL_CA + CONV_CH

VMEM_LIMIT_BYTES = 56 * 1024 * 1024
CONV_HALO = 32
FFN_HALO = 8
FFN_ROW_SUB = 512
FFN_COL_SUB = 512

F32 = jnp.float32
BF16 = jnp.bfloat16


def _params(*sem):
    return pltpu.CompilerParams(dimension_semantics=sem, vmem_limit_bytes=VMEM_LIMIT_BYTES)


def _tile(n, pref):
    t = min(n, pref)
    assert n % t == 0, (n, pref)
    return t


def _rmsnorm_kernel(x_ref, g_ref, o_ref):
    x = x_ref[...]
    ms = jnp.mean(x * x, axis=-1, keepdims=True)
    o_ref[...] = (x * lax.rsqrt(ms + EPS) * g_ref[...]).astype(o_ref.dtype)


def _rmsnorm(x, g):
    t, d = x.shape
    tr = _tile(t, 256)
    return pl.pallas_call(
        _rmsnorm_kernel,
        grid=(t // tr,),
        in_specs=[pl.BlockSpec((tr, d), lambda i: (i, 0)),
                  pl.BlockSpec((1, d), lambda i: (0, 0))],
        out_specs=pl.BlockSpec((tr, d), lambda i: (i, 0)),
        out_shape=jax.ShapeDtypeStruct((t, d), BF16),
        compiler_params=_params("arbitrary"),
        name="rmsnorm",
    )(x, g.reshape(1, d))


def _mm_kernel(a_ref, b_ref, o_ref):
    o_ref[...] = jnp.dot(a_ref[...], b_ref[...], preferred_element_type=F32).astype(o_ref.dtype)


def _matmul(a, b, layer, out_dtype):
    m, k = a.shape
    n = b.shape[2]
    tm, tn = _tile(m, 1024), _tile(n, 1024)
    return pl.pallas_call(
        _mm_kernel,
        grid=(m // tm, n // tn),
        in_specs=[pl.BlockSpec((tm, k), lambda i, j: (i, 0)),
                  pl.BlockSpec((None, k, tn), lambda i, j: (layer, 0, j))],
        out_specs=pl.BlockSpec((tm, tn), lambda i, j: (i, j)),
        out_shape=jax.ShapeDtypeStruct((m, n), out_dtype),
        compiler_params=_params("arbitrary", "arbitrary"),
        name="matmul",
    )(a, b)


def _mm_res_kernel(a_ref, b_ref, r_ref, o_ref):
    o_ref[...] = r_ref[...] + jnp.dot(a_ref[...], b_ref[...], preferred_element_type=F32)


def _matmul_residual(a, b, layer, res):
    m, k = a.shape
    n = b.shape[2]
    tm = _tile(m, 1024)
    tn = _tile(n, (2 * 1024 * 1024) // k)
    return pl.pallas_call(
        _mm_res_kernel,
        grid=(m // tm, n // tn),
        in_specs=[pl.BlockSpec((tm, k), lambda i, j: (i, 0)),
                  pl.BlockSpec((None, k, tn), lambda i, j: (layer, 0, j)),
                  pl.BlockSpec((tm, tn), lambda i, j: (i, j))],
        out_specs=pl.BlockSpec((tm, tn), lambda i, j: (i, j)),
        out_shape=jax.ShapeDtypeStruct((m, n), F32),
        compiler_params=_params("arbitrary", "arbitrary"),
        name="matmul_residual",
    )(a, b, res)


def _retention_kernel(q_ref, k_ref, v_ref, g_ref, mask_ref, qdec_ref, kdec_ref, cdec_ref, gn_ref,
                      o_ref, state_ref):
    @pl.when(pl.program_id(1) == 0)
    def _():
        state_ref[...] = jnp.zeros_like(state_ref)

    heads = range(RET_HEADS)
    q = [q_ref[:, h * RET_DK:(h + 1) * RET_DK] for h in heads]
    k = [k_ref[:, h * RET_DK:(h + 1) * RET_DK] for h in heads]
    v = [v_ref[:, h * RET_DV:(h + 1) * RET_DV] for h in heads]
    scores = [lax.dot_general(q[h], k[h], (((1,), (1,)), ((), ())), preferred_element_type=F32) * mask_ref[h]
              for h in heads]
    state = [state_ref[h] for h in heads]
    cross = [jnp.dot((q[h].astype(F32) * qdec_ref[h]).astype(BF16), state[h].astype(BF16),
                     preferred_element_type=F32) for h in heads]
    inner = [jnp.dot(scores[h].astype(BF16), v[h], preferred_element_type=F32) for h in heads]
    for h in heads:
        kd = (k[h].astype(F32) * kdec_ref[h]).astype(BF16)
        kv = lax.dot_general(kd, v[h], (((0,), (0,)), ((), ())), preferred_element_type=F32)
        state_ref[h] = state[h] * cdec_ref[h] + kv
    for h in heads:
        out = inner[h] + cross[h]
        ms = jnp.mean(out * out, axis=-1, keepdims=True)
        y = out * lax.rsqrt(ms + EPS) * gn_ref[h]
        g = g_ref[:, h * RET_DV:(h + 1) * RET_DV].astype(F32)
        o_ref[:, h * RET_DV:(h + 1) * RET_DV] = (g * jax.nn.sigmoid(g) * y).astype(o_ref.dtype)


def _retention_tables(blk):
    lg = jnp.log(1.0 - 2.0 ** (-5.0 - jnp.arange(RET_HEADS, dtype=F32)))
    pos = jnp.arange(blk, dtype=F32)
    dist = jnp.abs(pos[:, None] - pos[None, :])
    allowed = (jnp.arange(blk)[None, :] // CHUNK) <= (jnp.arange(blk)[:, None] // CHUNK)
    kscale = RET_DK ** -0.5
    mask = jnp.where(allowed[None], jnp.exp(dist[None] * lg[:, None, None]), 0.0) * kscale
    qdec = jnp.exp((pos[None, :] + 1.0) * lg[:, None])
    kdec = jnp.exp((blk - 1.0 - pos[None, :]) * lg[:, None]) * kscale
    cdec = jnp.exp(blk * lg)
    qdec = jnp.broadcast_to(qdec[:, :, None], (RET_HEADS, blk, RET_DK))
    kdec = jnp.broadcast_to(kdec[:, :, None], (RET_HEADS, blk, RET_DK))
    cdec = jnp.broadcast_to(cdec[:, None, None], (RET_HEADS, 1, RET_DV))
    return mask, qdec, kdec, cdec


def _retention(z, out_norm, bsz, seq):
    t = z.shape[0]
    blk = _tile(seq, 256)
    nblk = seq // blk
    mask, qdec, kdec, cdec = _retention_tables(blk)
    qk_w, v_w = RET_HEADS * RET_DK, RET_HEADS * RET_DV
    whole = lambda shape: pl.BlockSpec(shape, lambda b, i: (0,) * len(shape))
    return pl.pallas_call(
        _retention_kernel,
        grid=(bsz, nblk),
        in_specs=[
            pl.BlockSpec((blk, qk_w), lambda b, i: (b * nblk + i, COL_RQ // qk_w)),
            pl.BlockSpec((blk, qk_w), lambda b, i: (b * nblk + i, COL_RK // qk_w)),
            pl.BlockSpec((blk, v_w), lambda b, i: (b * nblk + i, COL_RV // v_w)),
            pl.BlockSpec((blk, v_w), lambda b, i: (b * nblk + i, COL_RG // v_w)),
            whole((RET_HEADS, blk, blk)),
            whole((RET_HEADS, blk, RET_DK)),
            whole((RET_HEADS, blk, RET_DK)),
            whole((RET_HEADS, 1, RET_DV)),
            whole((RET_HEADS, 1, RET_DV)),
        ],
        out_specs=pl.BlockSpec((blk, v_w), lambda b, i: (b * nblk + i, 0)),
        out_shape=jax.ShapeDtypeStruct((t, v_w), BF16),
        scratch_shapes=[pltpu.VMEM((RET_HEADS, RET_DK, RET_DV), F32)],
        compiler_params=_params("arbitrary", "arbitrary"),
        name="retention",
    )(z, z, z, z, mask, qdec, kdec, cdec, out_norm.reshape(RET_HEADS, 1, RET_DV))


def _qk_norm_kernel(q_ref, k_ref, gq_ref, gk_ref, qo_ref, ko_ref):
    scale = DIFF_DH ** -0.5 * LOG2E
    for src, g_ref, dst, mul in ((q_ref, gq_ref, qo_ref, scale), (k_ref, gk_ref, ko_ref, 1.0)):
        g = g_ref[...] * mul
        for c in range(src.shape[1] // DIFF_DH):
            x = src[:, c * DIFF_DH:(c + 1) * DIFF_DH].astype(F32)
            ms = jnp.mean(x * x, axis=-1, keepdims=True)
            dst[:, c * DIFF_DH:(c + 1) * DIFF_DH] = (x * lax.rsqrt(ms + EPS) * g).astype(dst.dtype)


def _qk_norm(z, gq, gk):
    t = z.shape[0]
    w = DIFF_HEADS * 2 * DIFF_DH
    tr = _tile(t, 512)
    spec = lambda col: pl.BlockSpec((tr, w), lambda i: (i, col // w))
    vec = pl.BlockSpec((1, DIFF_DH), lambda i: (0, 0))
    return pl.pallas_call(
        _qk_norm_kernel,
        grid=(t // tr,),
        in_specs=[spec(COL_DQ), spec(COL_DK), vec, vec],
        out_specs=[pl.BlockSpec((tr, w), lambda i: (i, 0))] * 2,
        out_shape=[jax.ShapeDtypeStruct((t, w), BF16)] * 2,
        compiler_params=_params("arbitrary"),
        name="qk_norm",
    )(z, z, gq.reshape(1, DIFF_DH), gk.reshape(1, DIFF_DH))


def _diff_attn_kernel(qt_ref, k_ref, vt_ref, nb_ref, slope_ref, lam_ref, gn_ref, o_ref,
                      m_sc, l_sc, acc_sc, *, blk, lam_init):
    i = pl.program_id(2)
    m_sc[...] = jnp.full_like(m_sc, NEG_INF)
    l_sc[...] = jnp.zeros_like(l_sc)
    acc_sc[...] = jnp.zeros_like(acc_sc)

    def accumulate(j, carry):
        start = pl.multiple_of(j * blk, blk)
        ks = k_ref[pl.ds(start, blk), :]
        vts = vt_ref[:, pl.ds(start, blk)]
        bias = nb_ref[0, (j == i).astype(jnp.int32)]
        soff = slope_ref[0] * ((i - j) * blk).astype(F32)
        s1s = []
        for m in range(2):
            st = jnp.dot(ks[:, m * DIFF_DH:(m + 1) * DIFF_DH], qt_ref[m * DIFF_DH:(m + 1) * DIFF_DH, :],
                         preferred_element_type=F32)
            s1s.append(st + bias)
        ps, alphas = [], []
        for m in range(2):
            m_old = m_sc[m]
            m_new = jnp.maximum(m_old, jnp.max(s1s[m], axis=0, keepdims=True) - soff)
            alpha = jnp.exp2(m_old - m_new)
            p = jnp.exp2(s1s[m] - (m_new + soff))
            l_sc[m] = alpha * l_sc[m] + jnp.sum(p, axis=0, keepdims=True)
            m_sc[m] = m_new
            ps.append(p.astype(BF16))
            alphas.append(alpha)
        for m in range(2):
            acc_sc[m] = alphas[m] * acc_sc[m] + jnp.dot(vts, ps[m], preferred_element_type=F32)
        return carry

    lax.fori_loop(0, i + 1, accumulate, 0)

    lv = lam_ref[...]
    lam = (jnp.exp(jnp.sum(lv[0:1] * lv[1:2], axis=-1, keepdims=True))
           - jnp.exp(jnp.sum(lv[2:3] * lv[3:4], axis=-1, keepdims=True)) + lam_init)
    out = acc_sc[0] / l_sc[0] - lam * (acc_sc[1] / l_sc[1])
    ms = jnp.mean(out * out, axis=0, keepdims=True)
    o_ref[...] = (out * lax.rsqrt(ms + EPS) * gn_ref[0] * (1.0 - lam_init)).astype(o_ref.dtype)


def _diff_attention(qt, kd, vt, lam_params, out_norm, lam_init, bsz, seq):
    t = kd.shape[0]
    hw = 2 * DIFF_DH
    blk = _tile(seq, 512)
    nblk = seq // blk
    slopes = 2.0 ** (-8.0 * (jnp.arange(DIFF_HEADS, dtype=F32) + 1.0) / DIFF_HEADS) * LOG2E
    r = jnp.arange(blk)[None, :]
    c = jnp.arange(blk)[:, None]
    past = -slopes[:, None, None] * (r - c).astype(F32)[None]
    allowed = (c // CHUNK) <= (r // CHUNK)
    diag = jnp.where(allowed[None], -slopes[:, None, None] * jnp.abs(r - c).astype(F32)[None], NEG_INF)
    nb = jnp.stack([past, diag], axis=1)
    slope_rows = jnp.broadcast_to(slopes[:, None, None], (DIFF_HEADS, 1, blk))
    gn = jnp.broadcast_to(out_norm.reshape(DIFF_HEADS, hw, 1), (DIFF_HEADS, hw, blk))
    kernel = functools.partial(_diff_attn_kernel, blk=blk, lam_init=lam_init)
    return pl.pallas_call(
        kernel,
        grid=(bsz, DIFF_HEADS, nblk),
        in_specs=[
            pl.BlockSpec((hw, blk), lambda b, h, i: (h, b * nblk + i)),
            pl.BlockSpec((seq, hw), lambda b, h, i: (b, h)),
            pl.BlockSpec((hw, seq), lambda b, h, i: (h, b)),
            pl.BlockSpec((1, 2, blk, blk), lambda b, h, i: (h, 0, 0, 0)),
            pl.BlockSpec((1, 1, blk), lambda b, h, i: (h, 0, 0)),
            pl.BlockSpec((4, DIFF_DH), lambda b, h, i: (0, 0)),
            pl.BlockSpec((1, hw, blk), lambda b, h, i: (h, 0, 0)),
        ],
        out_specs=pl.BlockSpec((hw, blk), lambda b, h, i: (h, b * nblk + i)),
        out_shape=jax.ShapeDtypeStruct((DIFF_HEADS * hw, t), BF16),
        scratch_shapes=[pltpu.VMEM((2, 1, blk), F32), pltpu.VMEM((2, 1, blk), F32),
                        pltpu.VMEM((2, hw, blk), F32)],
        compiler_params=_params("arbitrary", "arbitrary", "arbitrary"),
        name="diff_attention",
    )(qt, kd, vt, nb, slope_rows, lam_params, gn)


def _conv_module_kernel(ca_ref, cb_ref, cah_ref, cbh_ref, w_ref, b_ref, g_ref, beta_ref, o_ref, xe_ref,
                        xs_ref, *, rows, sub):
    first = pl.program_id(1) == 0
    halo = cah_ref[...].astype(F32) * jax.nn.sigmoid(cbh_ref[...].astype(F32))
    xe_ref[0:CONV_HALO, :] = jnp.where(first, 0.0, halo)
    xe_ref[CONV_HALO:CONV_HALO + rows, :] = ca_ref[...].astype(F32) * jax.nn.sigmoid(cb_ref[...].astype(F32))

    base = CONV_HALO - (CONV_K - 1)
    span = xs_ref.shape[1]
    for s in range(1, 8):
        xs_ref[s - 1] = xe_ref[s:s + span, :]

    for r0 in range(0, rows, sub):
        acc = jnp.broadcast_to(b_ref[...], (sub, CONV_CH))
        for k in range(CONV_K):
            a, s = divmod(base + k, 8)
            src = xe_ref if s == 0 else xs_ref.at[s - 1]
            acc = acc + jnp.tile(w_ref[k], (sub // 8, 1)) * src[r0 + 8 * a:r0 + 8 * a + sub, :]
        mu = jnp.mean(acc, axis=-1, keepdims=True)
        xc = acc - mu
        var = jnp.mean(xc * xc, axis=-1, keepdims=True)
        y = xc * lax.rsqrt(var + EPS) * g_ref[...] + beta_ref[...]
        o_ref[r0:r0 + sub, :] = (y * jax.nn.sigmoid(y)).astype(o_ref.dtype)


def _conv_module(z, w, b, ln_g, ln_b, bsz, seq):
    t = z.shape[0]
    rows = _tile(seq, 256)
    nblk = seq // rows
    hpb = rows // CONV_HALO
    main = lambda col: pl.BlockSpec((rows, CONV_CH), lambda bb, i: (bb * nblk + i, col // CONV_CH))
    halo = lambda col: pl.BlockSpec(
        (CONV_HALO, CONV_CH), lambda bb, i: (jnp.maximum((bb * nblk + i) * hpb - 1, 0), col // CONV_CH))
    vec = pl.BlockSpec((1, CONV_CH), lambda bb, i: (0, 0))
    kernel = functools.partial(_conv_module_kernel, rows=rows, sub=32)
    return pl.pallas_call(
        kernel,
        grid=(bsz, nblk),
        in_specs=[main(COL_CA), main(COL_CB), halo(COL_CA), halo(COL_CB),
                  pl.BlockSpec((CONV_K, 8, CONV_CH), lambda bb, i: (0, 0, 0)), vec, vec, vec],
        out_specs=pl.BlockSpec((rows, CONV_CH), lambda bb, i: (bb * nblk + i, 0)),
        out_shape=jax.ShapeDtypeStruct((t, CONV_CH), BF16),
        scratch_shapes=[pltpu.VMEM((CONV_HALO + rows, CONV_CH), F32),
                        pltpu.VMEM((7, CONV_HALO + rows - 8, CONV_CH), F32)],
        compiler_params=_params("arbitrary", "arbitrary"),
        name="conv_module",
    )(z, z, z, z, jnp.broadcast_to(w[:, None, :], (CONV_K, 8, CONV_CH)), b.reshape(1, CONV_CH),
      ln_g.reshape(1, CONV_CH), ln_b.reshape(1, CONV_CH))


def _merge_kernel(h_ref, wg0_ref, wg1_ref, wg2_ref, bg0_ref, bg1_ref, bg2_ref, ya_ref, yb_ref, yc_ref,
                  wbr_ref, o_ref):
    h = h_ref[...]
    merged = None
    for j, (wg_ref, bg_ref, y_ref) in enumerate(((wg0_ref, bg0_ref, ya_ref), (wg1_ref, bg1_ref, yb_ref),
                                                 (wg2_ref, bg2_ref, yc_ref))):
        gate = jax.nn.sigmoid(jnp.dot(h, wg_ref[...], preferred_element_type=F32) + bg_ref[...])
        term = gate * jnp.dot(y_ref[...], wbr_ref[j], preferred_element_type=F32)
        merged = term if merged is None else merged + term
    o_ref[...] = merged.astype(o_ref.dtype)


def _gated_merge(h, w_gate, b_gate, ya, yb, yc, w_br, layer):
    t, d = h.shape
    tm, tn = _tile(t, 512), _tile(d, 512)
    nd = d // tn
    ysp = pl.BlockSpec((tm, BRANCH_WIDTH), lambda n, m: (m, 0))
    wsp = [pl.BlockSpec((None, d, tn), lambda n, m, j=j: (layer, 0, j * nd + n)) for j in range(N_BRANCH)]
    bsp = [pl.BlockSpec((1, tn), lambda n, m, j=j: (0, j * nd + n)) for j in range(N_BRANCH)]
    return pl.pallas_call(
        _merge_kernel,
        grid=(nd, t // tm),
        in_specs=[pl.BlockSpec((tm, d), lambda n, m: (m, 0)), *wsp, *bsp, ysp, ysp, ysp,
                  pl.BlockSpec((None, N_BRANCH, BRANCH_WIDTH, tn), lambda n, m: (layer, 0, 0, n))],
        out_specs=pl.BlockSpec((tm, tn), lambda n, m: (m, n)),
        out_shape=jax.ShapeDtypeStruct((t, d), BF16),
        compiler_params=_params("arbitrary", "arbitrary"),
        name="gated_merge",
    )(h, w_gate, w_gate, w_gate, b_gate, b_gate, b_gate, ya, yb, yc, w_br)


def _ffn_in_kernel(h_ref, wg_ref, wu_ref, cw_ref, cb_ref, o_ref, fg_ref, *, tm, tn, rsub, csub,
                   tiles_per_seq):
    seq_start = pl.program_id(1) % tiles_per_seq == 0

    @pl.when(seq_start)
    def _():
        fg_ref[0:FFN_HALO, :] = jnp.zeros((FFN_HALO, tn), F32)

    @pl.when(jnp.logical_not(seq_start))
    def _():
        fg_ref[0:FFN_HALO, :] = fg_ref[tm:tm + FFN_HALO, :]

    def project(r0, c0):
        h = h_ref[r0:r0 + rsub, :]
        fg = jnp.dot(h, wg_ref[:, c0:c0 + csub], preferred_element_type=F32)
        fu = jnp.dot(h, wu_ref[:, c0:c0 + csub], preferred_element_type=F32)
        fg_ref[FFN_HALO + r0:FFN_HALO + r0 + rsub, c0:c0 + csub] = fg
        return fg, fu

    def finish(r0, c0, fg, fu):
        cols = slice(c0, c0 + csub)
        conv = (cw_ref[0:1, cols] * fg_ref[FFN_HALO - 2 + r0:FFN_HALO - 2 + r0 + rsub, cols]
                + cw_ref[1:2, cols] * fg_ref[FFN_HALO - 1 + r0:FFN_HALO - 1 + r0 + rsub, cols]
                + cw_ref[2:3, cols] * fg + cb_ref[:, cols])
        gelu = 0.5 * conv * (1.0 + lax.erf(conv * (2.0 ** -0.5)))
        o_ref[r0:r0 + rsub, cols] = (gelu * fu).astype(o_ref.dtype)

    pending = None
    for r0 in range(0, tm, rsub):
        for c0 in range(0, tn, csub):
            projected = project(r0, c0)
            if pending is not None:
                finish(*pending)
            pending = (r0, c0, *projected)
    finish(*pending)


def _ffn_in(h, w_in, layer, conv_w, conv_b, seq):
    t, d = h.shape
    f = w_in.shape[2] // 2
    tm, tn = _tile(seq, 1024), _tile(f, 512)
    rsub, csub = _tile(tm, FFN_ROW_SUB), _tile(tn, FFN_COL_SUB)
    nf = f // tn
    kernel = functools.partial(_ffn_in_kernel, tm=tm, tn=tn, rsub=rsub, csub=csub, tiles_per_seq=seq // tm)
    return pl.pallas_call(
        kernel,
        grid=(nf, t // tm),
        in_specs=[pl.BlockSpec((tm, d), lambda n, m: (m, 0)),
                  pl.BlockSpec((None, d, tn), lambda n, m: (layer, 0, n)),
                  pl.BlockSpec((None, d, tn), lambda n, m: (layer, 0, nf + n)),
                  pl.BlockSpec((FFN_CONV_K, tn), lambda n, m: (0, n)),
                  pl.BlockSpec((1, tn), lambda n, m: (0, n))],
        out_specs=pl.BlockSpec((tm, tn), lambda n, m: (m, n)),
        out_shape=jax.ShapeDtypeStruct((t, f), BF16),
        scratch_shapes=[pltpu.VMEM((FFN_HALO + tm, tn), F32)],
        compiler_params=_params("arbitrary", "arbitrary"),
        name="ffn_in",
    )(h, w_in, w_in, conv_w, conv_b.reshape(1, f))


def kernel(x, norm_mix, w_in, w_gate, b_gate, diff_q_norm, diff_k_norm, diff_lambda, diff_out_norm,
           ret_out_norm, conv_w, conv_b, conv_ln_g, conv_ln_b, w_br, w_o, norm_ffn, w_ffn_in,
           ffn_conv_w, ffn_conv_b, w_ffn_out):
    bsz, seq, d = x.shape
    depth = w_in.shape[0]
    t = bsz * seq
    x = x.reshape(t, d)
    w_in, w_gate, w_br, w_o, w_ffn_in, w_ffn_out = (
        w.astype(BF16) for w in (w_in, w_gate, w_br, w_o, w_ffn_in, w_ffn_out))
    for l in range(depth):
        lam_init = 0.8 - 0.6 * math.exp(-0.3 * l)
        h = _rmsnorm(x, norm_mix[l])
        z = _matmul(h, w_in, l, BF16)
        ya = _retention(z, ret_out_norm[l], bsz, seq)
        qd, kd = _qk_norm(z, diff_q_norm[l], diff_k_norm[l])
        vt = z[:, COL_DV:COL_DV + DIFF_HEADS * 2 * DIFF_DH].T
        yb = _diff_attention(qd.T, kd, vt, diff_lambda[l], diff_out_norm[l], lam_init, bsz, seq).T
        yc = _conv_module(z, conv_w[l], conv_b[l], conv_ln_g[l], conv_ln_b[l], bsz, seq)
        merged = _gated_merge(h, w_gate, b_gate[l].reshape(1, N_BRANCH * d), ya, yb, yc, w_br, l)
        x = _matmul_residual(merged, w_o, l, x)
        h = _rmsnorm(x, norm_ffn[l])
        act = _ffn_in(h, w_ffn_in, l, ffn_conv_w[l], ffn_conv_b[l], seq)
        x = _matmul_residual(act, w_ffn_out, l, x)
    return x.reshape(bsz, seq, d)
```

```python
import functools
import math

import jax
import jax.numpy as jnp
from jax import lax
from jax.experimental import pallas as pl
from jax.experimental.pallas import tpu as pltpu

EPS = 1e-6
NEG_INF = -1e30
CHUNK = 64
LOG2E = 1.4426950408889634

RET_HEADS = 4
RET_DK = 128
RET_DV = 256
DIFF_HEADS = 4
DIFF_DH = 128
CONV_CH = 1024
CONV_K = 31
FFN_CONV_K = 3
BRANCH_WIDTH = 1024
N_BRANCH = 3

COL_RQ = 0
COL_RK = COL_RQ + RET_HEADS * RET_DK
COL_RV = COL_RK + RET_HEADS * RET_DK
COL_RG = COL_RV + RET_HEADS * RET_DV
COL_DQ = COL_RG + RET_HEADS * RET_DV
COL_DK = COL_DQ + DIFF_HEADS * 2 * DIFF_DH
COL_DV = COL_DK + DIFF_HEADS * 2 * DIFF_DH
COL_CA = COL_DV + DIFF_HEADS * 2 * DIFF_DH
COL_CB = COL_CA + CONV_CH

LANES = 128
VMEM_LIMIT_BYTES = 56 * 1024 * 1024
CONV_HALO = 32
FFN_HALO = 8
FFN_ROW_SUB = 512
FFN_COL_SUB = 512

F32 = jnp.float32
BF16 = jnp.bfloat16


def _params(*sem):
    return pltpu.CompilerParams(dimension_semantics=sem, vmem_limit_bytes=VMEM_LIMIT_BYTES)


def _tile(n, pref):
    t = min(n, pref)
    assert n % t == 0, (n, pref)
    return t


def _lane_fold(v):
    out = v[:, 0:LANES]
    for c in range(1, v.shape[1] // LANES):
        out = out + v[:, c * LANES:(c + 1) * LANES]
    return out


def _rinv_from_folded(ss, d):
    total = jnp.sum(ss, axis=-1, keepdims=True)
    return jnp.broadcast_to(lax.rsqrt(total * (1.0 / d) + EPS), ss.shape)


def _row_scale(acc, rinv):
    return acc * jnp.tile(rinv, (1, acc.shape[1] // LANES))


def _norm_prep_kernel(x_ref, g_ref, xg_ref, rinv_ref):
    x = x_ref[...]
    xg_ref[...] = (x * g_ref[...]).astype(xg_ref.dtype)
    rinv_ref[...] = _rinv_from_folded(_lane_fold(x * x), x.shape[1])


def _norm_prep(x, g):
    t, d = x.shape
    tr = _tile(t, 256)
    return pl.pallas_call(
        _norm_prep_kernel,
        grid=(t // tr,),
        in_specs=[pl.BlockSpec((tr, d), lambda i: (i, 0)),
                  pl.BlockSpec((1, d), lambda i: (0, 0))],
        out_specs=[pl.BlockSpec((tr, d), lambda i: (i, 0)), pl.BlockSpec((tr, LANES), lambda i: (i, 0))],
        out_shape=[jax.ShapeDtypeStruct((t, d), BF16), jax.ShapeDtypeStruct((t, LANES), F32)],
        compiler_params=_params("arbitrary"),
        name="norm_prep",
    )(x, g.reshape(1, d))


def _mm_kernel(a_ref, rinv_ref, b_ref, o_ref):
    acc = jnp.dot(a_ref[...], b_ref[...], preferred_element_type=F32)
    o_ref[...] = _row_scale(acc, rinv_ref[...]).astype(o_ref.dtype)


def _matmul(a, rinv, b, layer, out_dtype):
    m, k = a.shape
    n = b.shape[2]
    tm, tn = _tile(m, 1024), _tile(n, 1024)
    return pl.pallas_call(
        _mm_kernel,
        grid=(m // tm, n // tn),
        in_specs=[pl.BlockSpec((tm, k), lambda i, j: (i, 0)),
                  pl.BlockSpec((tm, LANES), lambda i, j: (i, 0)),
                  pl.BlockSpec((None, k, tn), lambda i, j: (layer, 0, j))],
        out_specs=pl.BlockSpec((tm, tn), lambda i, j: (i, j)),
        out_shape=jax.ShapeDtypeStruct((m, n), out_dtype),
        compiler_params=_params("arbitrary", "arbitrary"),
        name="matmul",
    )(a, rinv, b)


def _mm_res_kernel(a_ref, b_ref, r_ref, o_ref):
    o_ref[...] = r_ref[...] + jnp.dot(a_ref[...], b_ref[...], preferred_element_type=F32)


def _mm_res_norm_kernel(a_ref, b_ref, r_ref, g_ref, o_ref, xg_ref, rinv_ref, ss_ref, *, d, rsub):
    @pl.when(pl.program_id(1) == 0)
    def _():
        ss_ref[...] = jnp.zeros_like(ss_ref)

    def project(r0):
        return r_ref[r0:r0 + rsub, :] + jnp.dot(a_ref[r0:r0 + rsub, :], b_ref[...], preferred_element_type=F32)

    def finish(r0, x):
        rows = slice(r0, r0 + rsub)
        o_ref[rows, :] = x
        xg_ref[rows, :] = (x * g_ref[...]).astype(xg_ref.dtype)
        ss = _lane_fold(x * x) + ss_ref[rows, :]
        ss_ref[rows, :] = ss
        rinv_ref[rows, :] = _rinv_from_folded(ss, d)

    pending = None
    for r0 in range(0, a_ref.shape[0], rsub):
        x = project(r0)
        if pending is not None:
            finish(*pending)
        pending = (r0, x)
    finish(*pending)


def _matmul_residual(a, b, layer, res, g_next=None):
    m, k = a.shape
    n = b.shape[2]
    tm = _tile(m, 1024)
    tn = _tile(n, (2 * 1024 * 1024) // k)
    in_specs = [pl.BlockSpec((tm, k), lambda i, j: (i, 0)),
                pl.BlockSpec((None, k, tn), lambda i, j: (layer, 0, j)),
                pl.BlockSpec((tm, tn), lambda i, j: (i, j))]
    tile = pl.BlockSpec((tm, tn), lambda i, j: (i, j))
    if g_next is None:
        return pl.pallas_call(
            _mm_res_kernel,
            grid=(m // tm, n // tn),
            in_specs=in_specs,
            out_specs=tile,
            out_shape=jax.ShapeDtypeStruct((m, n), F32),
            compiler_params=_params("arbitrary", "arbitrary"),
            name="matmul_residual",
        )(a, b, res)
    return pl.pallas_call(
        functools.partial(_mm_res_norm_kernel, d=n, rsub=_tile(tm, 512)),
        grid=(m // tm, n // tn),
        in_specs=in_specs + [pl.BlockSpec((1, tn), lambda i, j: (0, j))],
        out_specs=[tile, tile, pl.BlockSpec((tm, LANES), lambda i, j: (i, 0))],
        out_shape=[jax.ShapeDtypeStruct((m, n), F32), jax.ShapeDtypeStruct((m, n), BF16),
                   jax.ShapeDtypeStruct((m, LANES), F32)],
        scratch_shapes=[pltpu.VMEM((tm, LANES), F32)],
        compiler_params=_params("arbitrary", "arbitrary"),
        name="matmul_residual_norm",
    )(a, b, res, g_next.reshape(1, n))


def _retention_kernel(q_ref, k_ref, v_ref, g_ref, mask_ref, qdec_ref, kdec_ref, cdec_ref, gn_ref,
                      o_ref, state_ref):
    @pl.when(pl.program_id(1) == 0)
    def _():
        state_ref[...] = jnp.zeros_like(state_ref)

    heads = range(RET_HEADS)
    q = [q_ref[:, h * RET_DK:(h + 1) * RET_DK] for h in heads]
    k = [k_ref[:, h * RET_DK:(h + 1) * RET_DK] for h in heads]
    v = [v_ref[:, h * RET_DV:(h + 1) * RET_DV] for h in heads]
    scores = [lax.dot_general(q[h], k[h], (((1,), (1,)), ((), ())), preferred_element_type=F32) * mask_ref[h]
              for h in heads]
    state = [state_ref[h] for h in heads]
    cross = [jnp.dot((q[h].astype(F32) * qdec_ref[h]).astype(BF16), state[h].astype(BF16),
                     preferred_element_type=F32) for h in heads]
    inner = [jnp.dot(scores[h].astype(BF16), v[h], preferred_element_type=F32) for h in heads]
    for h in heads:
        kd = (k[h].astype(F32) * kdec_ref[h]).astype(BF16)
        kv = lax.dot_general(kd, v[h], (((0,), (0,)), ((), ())), preferred_element_type=F32)
        state_ref[h] = state[h] * cdec_ref[h] + kv
    for h in heads:
        out = inner[h] + cross[h]
        ms = jnp.mean(out * out, axis=-1, keepdims=True)
        y = out * lax.rsqrt(ms + EPS) * gn_ref[h]
        g = g_ref[:, h * RET_DV:(h + 1) * RET_DV].astype(F32)
        o_ref[:, h * RET_DV:(h + 1) * RET_DV] = (g * jax.nn.sigmoid(g) * y).astype(o_ref.dtype)


def _retention_tables(blk):
    lg = jnp.log(1.0 - 2.0 ** (-5.0 - jnp.arange(RET_HEADS, dtype=F32)))
    pos = jnp.arange(blk, dtype=F32)
    dist = jnp.abs(pos[:, None] - pos[None, :])
    allowed = (jnp.arange(blk)[None, :] // CHUNK) <= (jnp.arange(blk)[:, None] // CHUNK)
    kscale = RET_DK ** -0.5
    mask = jnp.where(allowed[None], jnp.exp(dist[None] * lg[:, None, None]), 0.0) * kscale
    qdec = jnp.exp((pos[None, :] + 1.0) * lg[:, None])
    kdec = jnp.exp((blk - 1.0 - pos[None, :]) * lg[:, None]) * kscale
    cdec = jnp.exp(blk * lg)
    qdec = jnp.broadcast_to(qdec[:, :, None], (RET_HEADS, blk, RET_DK))
    kdec = jnp.broadcast_to(kdec[:, :, None], (RET_HEADS, blk, RET_DK))
    cdec = jnp.broadcast_to(cdec[:, None, None], (RET_HEADS, 1, RET_DV))
    return mask, qdec, kdec, cdec


def _retention(z, out_norm, bsz, seq):
    t = z.shape[0]
    blk = _tile(seq, 256)
    nblk = seq // blk
    mask, qdec, kdec, cdec = _retention_tables(blk)
    qk_w, v_w = RET_HEADS * RET_DK, RET_HEADS * RET_DV
    whole = lambda shape: pl.BlockSpec(shape, lambda b, i: (0,) * len(shape))
    return pl.pallas_call(
        _retention_kernel,
        grid=(bsz, nblk),
        in_specs=[
            pl.BlockSpec((blk, qk_w), lambda b, i: (b * nblk + i, COL_RQ // qk_w)),
            pl.BlockSpec((blk, qk_w), lambda b, i: (b * nblk + i, COL_RK // qk_w)),
            pl.BlockSpec((blk, v_w), lambda b, i: (b * nblk + i, COL_RV // v_w)),
            pl.BlockSpec((blk, v_w), lambda b, i: (b * nblk + i, COL_RG // v_w)),
            whole((RET_HEADS, blk, blk)),
            whole((RET_HEADS, blk, RET_DK)),
            whole((RET_HEADS, blk, RET_DK)),
            whole((RET_HEADS, 1, RET_DV)),
            whole((RET_HEADS, 1, RET_DV)),
        ],
        out_specs=pl.BlockSpec((blk, v_w), lambda b, i: (b * nblk + i, 0)),
        out_shape=jax.ShapeDtypeStruct((t, v_w), BF16),
        scratch_shapes=[pltpu.VMEM((RET_HEADS, RET_DK, RET_DV), F32)],
        compiler_params=_params("arbitrary", "arbitrary"),
        name="retention",
    )(z, z, z, z, mask, qdec, kdec, cdec, out_norm.reshape(RET_HEADS, 1, RET_DV))


def _qk_norm_kernel(q_ref, k_ref, v_ref, gq_ref, gk_ref, qt_ref, ko_ref, vt_ref):
    gq = gq_ref[...] * (DIFF_DH ** -0.5 * LOG2E)
    gk = gk_ref[...]
    for c in range(q_ref.shape[1] // DIFF_DH):
        cols = slice(c * DIFF_DH, (c + 1) * DIFF_DH)
        q = q_ref[:, cols].astype(F32)
        q = q * lax.rsqrt(jnp.mean(q * q, axis=-1, keepdims=True) + EPS) * gq
        qt_ref[cols, :] = q.T.astype(qt_ref.dtype)
        k = k_ref[:, cols].astype(F32)
        k = k * lax.rsqrt(jnp.mean(k * k, axis=-1, keepdims=True) + EPS) * gk
        ko_ref[:, cols] = k.astype(ko_ref.dtype)
        vt_ref[cols, :] = v_ref[:, cols].astype(F32).T.astype(vt_ref.dtype)


def _qk_norm(z, gq, gk):
    t = z.shape[0]
    w = DIFF_HEADS * 2 * DIFF_DH
    tr = _tile(t, 512)
    spec = lambda col: pl.BlockSpec((tr, w), lambda i: (i, col // w))
    vec = pl.BlockSpec((1, DIFF_DH), lambda i: (0, 0))
    rows = pl.BlockSpec((tr, w), lambda i: (i, 0))
    cols = pl.BlockSpec((w, tr), lambda i: (0, i))
    return pl.pallas_call(
        _qk_norm_kernel,
        grid=(t // tr,),
        in_specs=[spec(COL_DQ), spec(COL_DK), spec(COL_DV), vec, vec],
        out_specs=[cols, rows, cols],
        out_shape=[jax.ShapeDtypeStruct((w, t), BF16), jax.ShapeDtypeStruct((t, w), BF16),
                   jax.ShapeDtypeStruct((w, t), BF16)],
        compiler_params=_params("arbitrary"),
        name="qk_norm",
    )(z, z, z, gq.reshape(1, DIFF_DH), gk.reshape(1, DIFF_DH))


def _diff_attn_kernel(qt_ref, k_ref, vt_ref, nb_ref, slope_ref, lam_ref, gn_ref, o_ref,
                      m_sc, l_sc, acc_sc, p_sc, a_sc, *, blk, lam_init):
    i = pl.program_id(2)
    m_sc[...] = jnp.full_like(m_sc, NEG_INF)
    l_sc[...] = jnp.zeros_like(l_sc)
    acc_sc[...] = jnp.zeros_like(acc_sc)

    def scores(j):
        ks = k_ref[pl.ds(pl.multiple_of(j * blk, blk), blk), :]
        bias = nb_ref[0, (j == i).astype(jnp.int32)]
        return [jnp.dot(ks[:, m * DIFF_DH:(m + 1) * DIFF_DH], qt_ref[m * DIFF_DH:(m + 1) * DIFF_DH, :],
                        preferred_element_type=F32) + bias for m in range(2)]

    def softmax_update(j, slot, s1s):
        soff = slope_ref[0] * ((i - j) * blk).astype(F32)
        for m in range(2):
            m_old = m_sc[m]
            m_new = jnp.maximum(m_old, jnp.max(s1s[m], axis=0, keepdims=True) - soff)
            alpha = jnp.exp2(m_old - m_new)
            p = jnp.exp2(s1s[m] - (m_new + soff))
            l_sc[m] = alpha * l_sc[m] + jnp.sum(p, axis=0, keepdims=True)
            m_sc[m] = m_new
            p_sc[slot, m] = p.astype(BF16)
            a_sc[slot, m] = alpha

    def add_values(j, slot):
        vts = vt_ref[:, pl.ds(pl.multiple_of(j * blk, blk), blk)]
        for m in range(2):
            acc_sc[m] = a_sc[slot, m] * acc_sc[m] + jnp.dot(vts, p_sc[slot, m], preferred_element_type=F32)

    def step(j, slot):
        s1s = scores(j)
        add_values(j - 1, 1 - slot)
        softmax_update(j, slot, s1s)

    softmax_update(0, 0, scores(0))

    def pair(t, carry):
        step(2 * t + 1, 1)
        step(2 * t + 2, 0)
        return carry

    lax.fori_loop(0, i // 2, pair, 0)

    @pl.when(i % 2 == 1)
    def _():
        step(i, 1)
        add_values(i, 1)

    @pl.when(i % 2 == 0)
    def _():
        add_values(i, 0)

    lv = lam_ref[...]
    lam = (jnp.exp(jnp.sum(lv[0:1] * lv[1:2], axis=-1, keepdims=True))
           - jnp.exp(jnp.sum(lv[2:3] * lv[3:4], axis=-1, keepdims=True)) + lam_init)
    out = acc_sc[0] / l_sc[0] - lam * (acc_sc[1] / l_sc[1])
    ms = jnp.mean(out * out, axis=0, keepdims=True)
    y = out * lax.rsqrt(ms + EPS) * gn_ref[0] * (1.0 - lam_init)
    o_ref[...] = y.T.astype(o_ref.dtype)


def _diff_attention(qt, kd, vt, lam_params, out_norm, lam_init, bsz, seq):
    t = kd.shape[0]
    hw = 2 * DIFF_DH
    blk = _tile(seq, 512)
    nblk = seq // blk
    slopes = 2.0 ** (-8.0 * (jnp.arange(DIFF_HEADS, dtype=F32) + 1.0) / DIFF_HEADS) * LOG2E
    r = jnp.arange(blk)[None, :]
    c = jnp.arange(blk)[:, None]
    past = -slopes[:, None, None] * (r - c).astype(F32)[None]
    allowed = (c // CHUNK) <= (r // CHUNK)
    diag = jnp.where(allowed[None], -slopes[:, None, None] * jnp.abs(r - c).astype(F32)[None], NEG_INF)
    nb = jnp.stack([past, diag], axis=1)
    slope_rows = jnp.broadcast_to(slopes[:, None, None], (DIFF_HEADS, 1, blk))
    gn = jnp.broadcast_to(out_norm.reshape(DIFF_HEADS, hw, 1), (DIFF_HEADS, hw, blk))
    kernel = functools.partial(_diff_attn_kernel, blk=blk, lam_init=lam_init)
    return pl.pallas_call(
        kernel,
        grid=(bsz, DIFF_HEADS, nblk),
        in_specs=[
            pl.BlockSpec((hw, blk), lambda b, h, i: (h, b * nblk + i)),
            pl.BlockSpec((seq, hw), lambda b, h, i: (b, h)),
            pl.BlockSpec((hw, seq), lambda b, h, i: (h, b)),
            pl.BlockSpec((1, 2, blk, blk), lambda b, h, i: (h, 0, 0, 0)),
            pl.BlockSpec((1, 1, blk), lambda b, h, i: (h, 0, 0)),
            pl.BlockSpec((4, DIFF_DH), lambda b, h, i: (0, 0)),
            pl.BlockSpec((1, hw, blk), lambda b, h, i: (h, 0, 0)),
        ],
        out_specs=pl.BlockSpec((blk, hw), lambda b, h, i: (b * nblk + i, h)),
        out_shape=jax.ShapeDtypeStruct((t, DIFF_HEADS * hw), BF16),
        scratch_shapes=[pltpu.VMEM((2, 1, blk), F32), pltpu.VMEM((2, 1, blk), F32),
                        pltpu.VMEM((2, hw, blk), F32), pltpu.VMEM((2, 2, blk, blk), BF16),
                        pltpu.VMEM((2, 2, 1, blk), F32)],
        compiler_params=_params("arbitrary", "arbitrary", "arbitrary"),
        name="diff_attention",
    )(qt, kd, vt, nb, slope_rows, lam_params, gn)


def _conv_module_kernel(ca_ref, cb_ref, cah_ref, cbh_ref, w_ref, b_ref, g_ref, beta_ref, o_ref, xe_ref,
                        xs_ref, *, rows, sub):
    first = pl.program_id(1) == 0
    halo = cah_ref[...].astype(F32) * jax.nn.sigmoid(cbh_ref[...].astype(F32))
    xe_ref[0:CONV_HALO, :] = jnp.where(first, 0.0, halo)
    xe_ref[CONV_HALO:CONV_HALO + rows, :] = ca_ref[...].astype(F32) * jax.nn.sigmoid(cb_ref[...].astype(F32))

    base = CONV_HALO - (CONV_K - 1)
    span = xs_ref.shape[1]
    for s in range(1, 8):
        xs_ref[s - 1] = xe_ref[s:s + span, :]

    for r0 in range(0, rows, sub):
        acc = jnp.broadcast_to(b_ref[...], (sub, CONV_CH))
        for k in range(CONV_K):
            a, s = divmod(base + k, 8)
            src = xe_ref if s == 0 else xs_ref.at[s - 1]
            acc = acc + jnp.tile(w_ref[k], (sub // 8, 1)) * src[r0 + 8 * a:r0 + 8 * a + sub, :]
        mu = jnp.mean(acc, axis=-1, keepdims=True)
        xc = acc - mu
        var = jnp.mean(xc * xc, axis=-1, keepdims=True)
        y = xc * lax.rsqrt(var + EPS) * g_ref[...] + beta_ref[...]
        o_ref[r0:r0 + sub, :] = (y * jax.nn.sigmoid(y)).astype(o_ref.dtype)


def _conv_module(z, w, b, ln_g, ln_b, bsz, seq):
    t = z.shape[0]
    rows = _tile(seq, 256)
    nblk = seq // rows
    hpb = rows // CONV_HALO
    main = lambda col: pl.BlockSpec((rows, CONV_CH), lambda bb, i: (bb * nblk + i, col // CONV_CH))
    halo = lambda col: pl.BlockSpec(
        (CONV_HALO, CONV_CH), lambda bb, i: (jnp.maximum((bb * nblk + i) * hpb - 1, 0), col // CONV_CH))
    vec = pl.BlockSpec((1, CONV_CH), lambda bb, i: (0, 0))
    kernel = functools.partial(_conv_module_kernel, rows=rows, sub=32)
    return pl.pallas_call(
        kernel,
        grid=(bsz, nblk),
        in_specs=[main(COL_CA), main(COL_CB), halo(COL_CA), halo(COL_CB),
                  pl.BlockSpec((CONV_K, 8, CONV_CH), lambda bb, i: (0, 0, 0)), vec, vec, vec],
        out_specs=pl.BlockSpec((rows, CONV_CH), lambda bb, i: (bb * nblk + i, 0)),
        out_shape=jax.ShapeDtypeStruct((t, CONV_CH), BF16),
        scratch_shapes=[pltpu.VMEM((CONV_HALO + rows, CONV_CH), F32),
                        pltpu.VMEM((7, CONV_HALO + rows - 8, CONV_CH), F32)],
        compiler_params=_params("arbitrary", "arbitrary"),
        name="conv_module",
    )(z, z, z, z, jnp.broadcast_to(w[:, None, :], (CONV_K, 8, CONV_CH)), b.reshape(1, CONV_CH),
      ln_g.reshape(1, CONV_CH), ln_b.reshape(1, CONV_CH))


def _merge_kernel(h_ref, rinv_ref, wg0_ref, wg1_ref, wg2_ref, bg0_ref, bg1_ref, bg2_ref, ya_ref, yb_ref,
                  yc_ref, wbr_ref, o_ref):
    h = h_ref[...]
    rinv = jnp.tile(rinv_ref[...], (1, o_ref.shape[1] // LANES))
    merged = None
    for j, (wg_ref, bg_ref, y_ref) in enumerate(((wg0_ref, bg0_ref, ya_ref), (wg1_ref, bg1_ref, yb_ref),
                                                 (wg2_ref, bg2_ref, yc_ref))):
        gate = jax.nn.sigmoid(jnp.dot(h, wg_ref[...], preferred_element_type=F32) * rinv + bg_ref[...])
        term = gate * jnp.dot(y_ref[...], wbr_ref[j], preferred_element_type=F32)
        merged = term if merged is None else merged + term
    o_ref[...] = merged.astype(o_ref.dtype)


def _gated_merge(h, rinv, w_gate, b_gate, ya, yb, yc, w_br, layer):
    t, d = h.shape
    tm, tn = _tile(t, 512), _tile(d, 512)
    nd = d // tn
    ysp = pl.BlockSpec((tm, BRANCH_WIDTH), lambda n, m: (m, 0))
    wsp = [pl.BlockSpec((None, d, tn), lambda n, m, j=j: (layer, 0, j * nd + n)) for j in range(N_BRANCH)]
    bsp = [pl.BlockSpec((1, tn), lambda n, m, j=j: (0, j * nd + n)) for j in range(N_BRANCH)]
    return pl.pallas_call(
        _merge_kernel,
        grid=(nd, t // tm),
        in_specs=[pl.BlockSpec((tm, d), lambda n, m: (m, 0)), pl.BlockSpec((tm, LANES), lambda n, m: (m, 0)),
                  *wsp, *bsp, ysp, ysp, ysp,
                  pl.BlockSpec((None, N_BRANCH, BRANCH_WIDTH, tn), lambda n, m: (layer, 0, 0, n))],
        out_specs=pl.BlockSpec((tm, tn), lambda n, m: (m, n)),
        out_shape=jax.ShapeDtypeStruct((t, d), BF16),
        compiler_params=_params("arbitrary", "arbitrary"),
        name="gated_merge",
    )(h, rinv, w_gate, w_gate, w_gate, b_gate, b_gate, b_gate, ya, yb, yc, w_br)


def _ffn_in_kernel(h_ref, rinv_ref, wg_ref, wu_ref, cw_ref, cb_ref, o_ref, fg_ref, *, tm, tn, rsub, csub,
                   tiles_per_seq):
    seq_start = pl.program_id(1) % tiles_per_seq == 0

    @pl.when(seq_start)
    def _():
        fg_ref[0:FFN_HALO, :] = jnp.zeros((FFN_HALO, tn), F32)

    @pl.when(jnp.logical_not(seq_start))
    def _():
        fg_ref[0:FFN_HALO, :] = fg_ref[tm:tm + FFN_HALO, :]

    def project(r0, c0):
        h = h_ref[r0:r0 + rsub, :]
        rinv = rinv_ref[r0:r0 + rsub, :]
        fg = _row_scale(jnp.dot(h, wg_ref[:, c0:c0 + csub], preferred_element_type=F32), rinv)
        fu = _row_scale(jnp.dot(h, wu_ref[:, c0:c0 + csub], preferred_element_type=F32), rinv)
        fg_ref[FFN_HALO + r0:FFN_HALO + r0 + rsub, c0:c0 + csub] = fg
        return fg, fu

    def finish(r0, c0, fg, fu):
        cols = slice(c0, c0 + csub)
        conv = (cw_ref[0:1, cols] * fg_ref[FFN_HALO - 2 + r0:FFN_HALO - 2 + r0 + rsub, cols]
                + cw_ref[1:2, cols] * fg_ref[FFN_HALO - 1 + r0:FFN_HALO - 1 + r0 + rsub, cols]
                + cw_ref[2:3, cols] * fg + cb_ref[:, cols])
        gelu = 0.5 * conv * (1.0 + lax.erf(conv * (2.0 ** -0.5)))
        o_ref[r0:r0 + rsub, cols] = (gelu * fu).astype(o_ref.dtype)

    pending = None
    for r0 in range(0, tm, rsub):
        for c0 in range(0, tn, csub):
            projected = project(r0, c0)
            if pending is not None:
                finish(*pending)
            pending = (r0, c0, *projected)
    finish(*pending)


def _ffn_in(h, rinv, w_in, layer, conv_w, conv_b, seq):
    t, d = h.shape
    f = w_in.shape[2] // 2
    tm, tn = _tile(seq, 1024), _tile(f, 512)
    rsub, csub = _tile(tm, FFN_ROW_SUB), _tile(tn, FFN_COL_SUB)
    nf = f // tn
    kernel = functools.partial(_ffn_in_kernel, tm=tm, tn=tn, rsub=rsub, csub=csub, tiles_per_seq=seq // tm)
    return pl.pallas_call(
        kernel,
        grid=(nf, t // tm),
        in_specs=[pl.BlockSpec((tm, d), lambda n, m: (m, 0)),
                  pl.BlockSpec((tm, LANES), lambda n, m: (m, 0)),
                  pl.BlockSpec((None, d, tn), lambda n, m: (layer, 0, n)),
                  pl.BlockSpec((None, d, tn), lambda n, m: (layer, 0, nf + n)),
                  pl.BlockSpec((FFN_CONV_K, tn), lambda n, m: (0, n)),
                  pl.BlockSpec((1, tn), lambda n, m: (0, n))],
        out_specs=pl.BlockSpec((tm, tn), lambda n, m: (m, n)),
        out_shape=jax.ShapeDtypeStruct((t, f), BF16),
        scratch_shapes=[pltpu.VMEM((FFN_HALO + tm, tn), F32)],
        compiler_params=_params("arbitrary", "arbitrary"),
        name="ffn_in",
    )(h, rinv, w_in, w_in, conv_w, conv_b.reshape(1, f))


def kernel(x, norm_mix, w_in, w_gate, b_gate, diff_q_norm, diff_k_norm, diff_lambda, diff_out_norm,
           ret_out_norm, conv_w, conv_b, conv_ln_g, conv_ln_b, w_br, w_o, norm_ffn, w_ffn_in,
           ffn_conv_w, ffn_conv_b, w_ffn_out):
    bsz, seq, d = x.shape
    depth = w_in.shape[0]
    t = bsz * seq
    x = x.reshape(t, d)
    w_in, w_gate, w_br, w_o, w_ffn_in, w_ffn_out = (
        w.astype(BF16) for w in (w_in, w_gate, w_br, w_o, w_ffn_in, w_ffn_out))
    h, rinv = _norm_prep(x, norm_mix[0])
    for l in range(depth):
        lam_init = 0.8 - 0.6 * math.exp(-0.3 * l)
        z = _matmul(h, rinv, w_in, l, BF16)
        ya = _retention(z, ret_out_norm[l], bsz, seq)
        qt, kd, vt = _qk_norm(z, diff_q_norm[l], diff_k_norm[l])
        yb = _diff_attention(qt, kd, vt, diff_lambda[l], diff_out_norm[l], lam_init, bsz, seq)
        yc = _conv_module(z, conv_w[l], conv_b[l], conv_ln_g[l], conv_ln_b[l], bsz, seq)
        merged = _gated_merge(h, rinv, w_gate, b_gate[l].reshape(1, N_BRANCH * d), ya, yb, yc, w_br, l)
        x, h, rinv = _matmul_residual(merged, w_o, l, x, norm_ffn[l])
        act = _ffn_in(h, rinv, w_ffn_in, l, ffn_conv_w[l], ffn_conv_b[l], seq)
        if l + 1 < depth:
            x, h, rinv = _matmul_residual(act, w_ffn_out, l, x, norm_mix[l + 1])
        else:
            x = _matmul_residual(act, w_ffn_out, l, x)
    return x.reshape(bsz, seq, d)
```

```python
import functools
import math

import jax
import jax.numpy as jnp
from jax import lax
from jax.experimental import pallas as pl
from jax.experimental.pallas import tpu as pltpu

EPS = 1e-6
NEG_INF = -1e30
CHUNK = 64
LOG2E = 1.4426950408889634

RET_HEADS = 4
RET_DK = 128
RET_DV = 256
DIFF_HEADS = 4
DIFF_DH = 128
CONV_CH = 1024
CONV_K = 31
FFN_CONV_K = 3
BRANCH_WIDTH = 1024
N_BRANCH = 3

COL_RQ = 0
COL_RK = COL_RQ + RET_HEADS * RET_DK
COL_RV = COL_RK + RET_HEADS * RET_DK
COL_RG = COL_RV + RET_HEADS * RET_DV
COL_DQ = COL_RG + RET_HEADS * RET_DV
COL_DK = COL_DQ + DIFF_HEADS * 2 * DIFF_DH
COL_DV = COL_DK + DIFF_HEADS * 2 * DIFF_DH
COL_CA = COL_DV + DIFF_HEADS * 2 * DIFF_DH
COL_CB = COL_CA + CONV_CH

LANES = 128
CAST_ROWS = 16
VMEM_LIMIT_BYTES = 56 * 1024 * 1024
CONV_HALO = 32
FFN_HALO = 8
FFN_ROW_SUB = 512
FFN_COL_SUB = 512

F32 = jnp.float32
BF16 = jnp.bfloat16


def _params(*sem):
    return pltpu.CompilerParams(dimension_semantics=sem, vmem_limit_bytes=VMEM_LIMIT_BYTES)


def _tile(n, pref):
    t = min(n, pref)
    assert n % t == 0, (n, pref)
    return t


def _pallas_with_casts(kernel, args, *, grid, in_specs, out_specs, out_shape, name, scratch_shapes=(), casts=()):
    out_specs, out_shape = list(out_specs), list(out_shape)
    n_in, n_out, n_cast = len(in_specs), len(out_shape), len(casts)
    inner = grid[1]
    rows = grid[0] * grid[1] * CAST_ROWS
    views = []
    for w, layer in casts:
        per_layer = math.prod(w.shape[1:])
        assert per_layer % (rows * LANES) == 0, (w.shape, grid)
        views.append((w.reshape(w.shape[0], rows, per_layer // rows), layer))
    slab_in = [pl.BlockSpec((None, CAST_ROWS, v.shape[2]), lambda i, j, layer=layer: (layer, i * inner + j, 0))
               for v, layer in views]
    slab_out = [pl.BlockSpec((CAST_ROWS, v.shape[2]), lambda i, j: (i * inner + j, 0)) for v, _ in views]

    def body(*refs):
        ins, refs = refs[:n_in], refs[n_in:]
        cast_in, refs = refs[:n_cast], refs[n_cast:]
        outs, refs = refs[:n_out], refs[n_out:]
        cast_out, scratch = refs[:n_cast], refs[n_cast:]
        kernel(*ins, *outs, *scratch)
        for src, dst in zip(cast_in, cast_out):
            dst[...] = src[...].astype(dst.dtype)

    res = pl.pallas_call(
        body,
        grid=grid,
        in_specs=[*in_specs, *slab_in],
        out_specs=[*out_specs, *slab_out],
        out_shape=[*out_shape, *[jax.ShapeDtypeStruct(v.shape[1:], BF16) for v, _ in views]],
        scratch_shapes=list(scratch_shapes),
        compiler_params=_params("arbitrary", "arbitrary"),
        name=name,
    )(*args, *[v for v, _ in views])
    converted = [c.reshape((1,) + w.shape[1:]) for c, (w, _) in zip(res[n_out:], casts)]
    return res[:n_out], converted


def _lane_fold(v):
    out = v[:, 0:LANES]
    for c in range(1, v.shape[1] // LANES):
        out = out + v[:, c * LANES:(c + 1) * LANES]
    return out


def _rinv_from_folded(ss, d):
    total = jnp.sum(ss, axis=-1, keepdims=True)
    return jnp.broadcast_to(lax.rsqrt(total * (1.0 / d) + EPS), ss.shape)


def _row_scale(acc, rinv):
    return acc * jnp.tile(rinv, (1, acc.shape[1] // LANES))


def _norm_prep_kernel(x_ref, g_ref, xg_ref, rinv_ref):
    x = x_ref[...]
    xg_ref[...] = (x * g_ref[...]).astype(xg_ref.dtype)
    rinv_ref[...] = _rinv_from_folded(_lane_fold(x * x), x.shape[1])


def _norm_prep(x, g):
    t, d = x.shape
    tr = _tile(t, 256)
    return pl.pallas_call(
        _norm_prep_kernel,
        grid=(t // tr,),
        in_specs=[pl.BlockSpec((tr, d), lambda i: (i, 0)),
                  pl.BlockSpec((1, d), lambda i: (0, 0))],
        out_specs=[pl.BlockSpec((tr, d), lambda i: (i, 0)), pl.BlockSpec((tr, LANES), lambda i: (i, 0))],
        out_shape=[jax.ShapeDtypeStruct((t, d), BF16), jax.ShapeDtypeStruct((t, LANES), F32)],
        compiler_params=_params("arbitrary"),
        name="norm_prep",
    )(x, g.reshape(1, d))


def _mm_kernel(a_ref, rinv_ref, b_ref, o_ref):
    acc = jnp.dot(a_ref[...], b_ref[...], preferred_element_type=F32)
    o_ref[...] = _row_scale(acc, rinv_ref[...]).astype(o_ref.dtype)


def _matmul(a, rinv, b, layer, out_dtype, casts=()):
    m, k = a.shape
    n = b.shape[2]
    tm, tn = _tile(m, 1024), _tile(n, 1024)
    (out,), converted = _pallas_with_casts(
        _mm_kernel, (a, rinv, b),
        grid=(m // tm, n // tn),
        in_specs=[pl.BlockSpec((tm, k), lambda i, j: (i, 0)),
                  pl.BlockSpec((tm, LANES), lambda i, j: (i, 0)),
                  pl.BlockSpec((None, k, tn), lambda i, j: (layer, 0, j))],
        out_specs=[pl.BlockSpec((tm, tn), lambda i, j: (i, j))],
        out_shape=[jax.ShapeDtypeStruct((m, n), out_dtype)],
        name="matmul", casts=casts)
    return out, converted


def _mm_res_kernel(a_ref, b_ref, r_ref, o_ref):
    o_ref[...] = r_ref[...] + jnp.dot(a_ref[...], b_ref[...], preferred_element_type=F32)


def _mm_res_norm_kernel(a_ref, b_ref, r_ref, g_ref, o_ref, xg_ref, rinv_ref, ss_ref, *, d, rsub):
    @pl.when(pl.program_id(1) == 0)
    def _():
        ss_ref[...] = jnp.zeros_like(ss_ref)

    def project(r0):
        return r_ref[r0:r0 + rsub, :] + jnp.dot(a_ref[r0:r0 + rsub, :], b_ref[...], preferred_element_type=F32)

    def finish(r0, x):
        rows = slice(r0, r0 + rsub)
        o_ref[rows, :] = x
        xg_ref[rows, :] = (x * g_ref[...]).astype(xg_ref.dtype)
        ss = _lane_fold(x * x) + ss_ref[rows, :]
        ss_ref[rows, :] = ss
        rinv_ref[rows, :] = _rinv_from_folded(ss, d)

    pending = None
    for r0 in range(0, a_ref.shape[0], rsub):
        x = project(r0)
        if pending is not None:
            finish(*pending)
        pending = (r0, x)
    finish(*pending)


def _matmul_residual(a, b, layer, res, g_next=None, casts=()):
    m, k = a.shape
    n = b.shape[2]
    tm = _tile(m, 1024)
    tn = _tile(n, (2 * 1024 * 1024) // k)
    in_specs = [pl.BlockSpec((tm, k), lambda i, j: (i, 0)),
                pl.BlockSpec((None, k, tn), lambda i, j: (layer, 0, j)),
                pl.BlockSpec((tm, tn), lambda i, j: (i, j))]
    tile = pl.BlockSpec((tm, tn), lambda i, j: (i, j))
    if g_next is None:
        return pl.pallas_call(
            _mm_res_kernel,
            grid=(m // tm, n // tn),
            in_specs=in_specs,
            out_specs=tile,
            out_shape=jax.ShapeDtypeStruct((m, n), F32),
            compiler_params=_params("arbitrary", "arbitrary"),
            name="matmul_residual",
        )(a, b, res)
    return _pallas_with_casts(
        functools.partial(_mm_res_norm_kernel, d=n, rsub=_tile(tm, 512)), (a, b, res, g_next.reshape(1, n)),
        grid=(m // tm, n // tn),
        in_specs=in_specs + [pl.BlockSpec((1, tn), lambda i, j: (0, j))],
        out_specs=[tile, tile, pl.BlockSpec((tm, LANES), lambda i, j: (i, 0))],
        out_shape=[jax.ShapeDtypeStruct((m, n), F32), jax.ShapeDtypeStruct((m, n), BF16),
                   jax.ShapeDtypeStruct((m, LANES), F32)],
        scratch_shapes=[pltpu.VMEM((tm, LANES), F32)],
        name="matmul_residual_norm", casts=casts)


def _retention_kernel(q_ref, k_ref, v_ref, g_ref, mask_ref, qdec_ref, kdec_ref, cdec_ref, gn_ref,
                      o_ref, state_ref):
    @pl.when(pl.program_id(1) == 0)
    def _():
        state_ref[...] = jnp.zeros_like(state_ref)

    heads = range(RET_HEADS)
    q = [q_ref[:, h * RET_DK:(h + 1) * RET_DK] for h in heads]
    k = [k_ref[:, h * RET_DK:(h + 1) * RET_DK] for h in heads]
    v = [v_ref[:, h * RET_DV:(h + 1) * RET_DV] for h in heads]
    scores = [lax.dot_general(q[h], k[h], (((1,), (1,)), ((), ())), preferred_element_type=F32) * mask_ref[h]
              for h in heads]
    state = [state_ref[h] for h in heads]
    cross = [jnp.dot((q[h].astype(F32) * qdec_ref[h]).astype(BF16), state[h].astype(BF16),
                     preferred_element_type=F32) for h in heads]
    inner = [jnp.dot(scores[h].astype(BF16), v[h], preferred_element_type=F32) for h in heads]
    for h in heads:
        kd = (k[h].astype(F32) * kdec_ref[h]).astype(BF16)
        kv = lax.dot_general(kd, v[h], (((0,), (0,)), ((), ())), preferred_element_type=F32)
        state_ref[h] = state[h] * cdec_ref[h] + kv
    for h in heads:
        out = inner[h] + cross[h]
        ms = jnp.mean(out * out, axis=-1, keepdims=True)
        y = out * lax.rsqrt(ms + EPS) * gn_ref[h]
        g = g_ref[:, h * RET_DV:(h + 1) * RET_DV].astype(F32)
        o_ref[:, h * RET_DV:(h + 1) * RET_DV] = (g * jax.nn.sigmoid(g) * y).astype(o_ref.dtype)


def _retention_tables(blk):
    lg = jnp.log(1.0 - 2.0 ** (-5.0 - jnp.arange(RET_HEADS, dtype=F32)))
    pos = jnp.arange(blk, dtype=F32)
    dist = jnp.abs(pos[:, None] - pos[None, :])
    allowed = (jnp.arange(blk)[None, :] // CHUNK) <= (jnp.arange(blk)[:, None] // CHUNK)
    kscale = RET_DK ** -0.5
    mask = jnp.where(allowed[None], jnp.exp(dist[None] * lg[:, None, None]), 0.0) * kscale
    qdec = jnp.exp((pos[None, :] + 1.0) * lg[:, None])
    kdec = jnp.exp((blk - 1.0 - pos[None, :]) * lg[:, None]) * kscale
    cdec = jnp.exp(blk * lg)
    qdec = jnp.broadcast_to(qdec[:, :, None], (RET_HEADS, blk, RET_DK))
    kdec = jnp.broadcast_to(kdec[:, :, None], (RET_HEADS, blk, RET_DK))
    cdec = jnp.broadcast_to(cdec[:, None, None], (RET_HEADS, 1, RET_DV))
    return mask, qdec, kdec, cdec


def _retention(z, out_norm, bsz, seq):
    t = z.shape[0]
    blk = _tile(seq, 256)
    nblk = seq // blk
    mask, qdec, kdec, cdec = _retention_tables(blk)
    qk_w, v_w = RET_HEADS * RET_DK, RET_HEADS * RET_DV
    whole = lambda shape: pl.BlockSpec(shape, lambda b, i: (0,) * len(shape))
    return pl.pallas_call(
        _retention_kernel,
        grid=(bsz, nblk),
        in_specs=[
            pl.BlockSpec((blk, qk_w), lambda b, i: (b * nblk + i, COL_RQ // qk_w)),
            pl.BlockSpec((blk, qk_w), lambda b, i: (b * nblk + i, COL_RK // qk_w)),
            pl.BlockSpec((blk, v_w), lambda b, i: (b * nblk + i, COL_RV // v_w)),
            pl.BlockSpec((blk, v_w), lambda b, i: (b * nblk + i, COL_RG // v_w)),
            whole((RET_HEADS, blk, blk)),
            whole((RET_HEADS, blk, RET_DK)),
            whole((RET_HEADS, blk, RET_DK)),
            whole((RET_HEADS, 1, RET_DV)),
            whole((RET_HEADS, 1, RET_DV)),
        ],
        out_specs=pl.BlockSpec((blk, v_w), lambda b, i: (b * nblk + i, 0)),
        out_shape=jax.ShapeDtypeStruct((t, v_w), BF16),
        scratch_shapes=[pltpu.VMEM((RET_HEADS, RET_DK, RET_DV), F32)],
        compiler_params=_params("arbitrary", "arbitrary"),
        name="retention",
    )(z, z, z, z, mask, qdec, kdec, cdec, out_norm.reshape(RET_HEADS, 1, RET_DV))


def _qk_norm_kernel(q_ref, k_ref, v_ref, gq_ref, gk_ref, qt_ref, ko_ref, vt_ref):
    gq = gq_ref[...] * (DIFF_DH ** -0.5 * LOG2E)
    gk = gk_ref[...]
    for c in range(q_ref.shape[1] // DIFF_DH):
        cols = slice(c * DIFF_DH, (c + 1) * DIFF_DH)
        q = q_ref[:, cols].astype(F32)
        q = q * lax.rsqrt(jnp.mean(q * q, axis=-1, keepdims=True) + EPS) * gq
        qt_ref[cols, :] = q.T.astype(qt_ref.dtype)
        k = k_ref[:, cols].astype(F32)
        k = k * lax.rsqrt(jnp.mean(k * k, axis=-1, keepdims=True) + EPS) * gk
        ko_ref[:, cols] = k.astype(ko_ref.dtype)
        vt_ref[cols, :] = v_ref[:, cols].astype(F32).T.astype(vt_ref.dtype)


def _qk_norm(z, gq, gk):
    t = z.shape[0]
    w = DIFF_HEADS * 2 * DIFF_DH
    tr = _tile(t, 512)
    spec = lambda col: pl.BlockSpec((tr, w), lambda i: (i, col // w))
    vec = pl.BlockSpec((1, DIFF_DH), lambda i: (0, 0))
    rows = pl.BlockSpec((tr, w), lambda i: (i, 0))
    cols = pl.BlockSpec((w, tr), lambda i: (0, i))
    return pl.pallas_call(
        _qk_norm_kernel,
        grid=(t // tr,),
        in_specs=[spec(COL_DQ), spec(COL_DK), spec(COL_DV), vec, vec],
        out_specs=[cols, rows, cols],
        out_shape=[jax.ShapeDtypeStruct((w, t), BF16), jax.ShapeDtypeStruct((t, w), BF16),
                   jax.ShapeDtypeStruct((w, t), BF16)],
        compiler_params=_params("arbitrary"),
        name="qk_norm",
    )(z, z, z, gq.reshape(1, DIFF_DH), gk.reshape(1, DIFF_DH))


def _diff_attn_kernel(qt_ref, k_ref, vt_ref, nb_ref, slope_ref, lam_ref, gn_ref, o_ref,
                      m_sc, l_sc, acc_sc, p_sc, a_sc, *, blk, lam_init):
    i = pl.program_id(2)
    m_sc[...] = jnp.full_like(m_sc, NEG_INF)
    l_sc[...] = jnp.zeros_like(l_sc)
    acc_sc[...] = jnp.zeros_like(acc_sc)

    def scores(j):
        ks = k_ref[pl.ds(pl.multiple_of(j * blk, blk), blk), :]
        bias = nb_ref[0, (j == i).astype(jnp.int32)]
        return [jnp.dot(ks[:, m * DIFF_DH:(m + 1) * DIFF_DH], qt_ref[m * DIFF_DH:(m + 1) * DIFF_DH, :],
                        preferred_element_type=F32) + bias for m in range(2)]

    def softmax_update(j, slot, s1s):
        soff = slope_ref[0] * ((i - j) * blk).astype(F32)
        for m in range(2):
            m_old = m_sc[m]
            m_new = jnp.maximum(m_old, jnp.max(s1s[m], axis=0, keepdims=True) - soff)
            alpha = jnp.exp2(m_old - m_new)
            p = jnp.exp2(s1s[m] - (m_new + soff))
            l_sc[m] = alpha * l_sc[m] + jnp.sum(p, axis=0, keepdims=True)
            m_sc[m] = m_new
            p_sc[slot, m] = p.astype(BF16)
            a_sc[slot, m] = alpha

    def add_values(j, slot):
        vts = vt_ref[:, pl.ds(pl.multiple_of(j * blk, blk), blk)]
        for m in range(2):
            acc_sc[m] = a_sc[slot, m] * acc_sc[m] + jnp.dot(vts, p_sc[slot, m], preferred_element_type=F32)

    def step(j, slot):
        s1s = scores(j)
        add_values(j - 1, 1 - slot)
        softmax_update(j, slot, s1s)

    softmax_update(0, 0, scores(0))

    def pair(t, carry):
        step(2 * t + 1, 1)
        step(2 * t + 2, 0)
        return carry

    lax.fori_loop(0, i // 2, pair, 0)

    @pl.when(i % 2 == 1)
    def _():
        step(i, 1)
        add_values(i, 1)

    @pl.when(i % 2 == 0)
    def _():
        add_values(i, 0)

    lv = lam_ref[...]
    lam = (jnp.exp(jnp.sum(lv[0:1] * lv[1:2], axis=-1, keepdims=True))
           - jnp.exp(jnp.sum(lv[2:3] * lv[3:4], axis=-1, keepdims=True)) + lam_init)
    out = acc_sc[0] / l_sc[0] - lam * (acc_sc[1] / l_sc[1])
    ms = jnp.mean(out * out, axis=0, keepdims=True)
    y = out * lax.rsqrt(ms + EPS) * gn_ref[0] * (1.0 - lam_init)
    o_ref[...] = y.T.astype(o_ref.dtype)


def _diff_attention(qt, kd, vt, lam_params, out_norm, lam_init, bsz, seq):
    t = kd.shape[0]
    hw = 2 * DIFF_DH
    blk = _tile(seq, 512)
    nblk = seq // blk
    slopes = 2.0 ** (-8.0 * (jnp.arange(DIFF_HEADS, dtype=F32) + 1.0) / DIFF_HEADS) * LOG2E
    r = jnp.arange(blk)[None, :]
    c = jnp.arange(blk)[:, None]
    past = -slopes[:, None, None] * (r - c).astype(F32)[None]
    allowed = (c // CHUNK) <= (r // CHUNK)
    diag = jnp.where(allowed[None], -slopes[:, None, None] * jnp.abs(r - c).astype(F32)[None], NEG_INF)
    nb = jnp.stack([past, diag], axis=1)
    slope_rows = jnp.broadcast_to(slopes[:, None, None], (DIFF_HEADS, 1, blk))
    gn = jnp.broadcast_to(out_norm.reshape(DIFF_HEADS, hw, 1), (DIFF_HEADS, hw, blk))
    kernel = functools.partial(_diff_attn_kernel, blk=blk, lam_init=lam_init)
    return pl.pallas_call(
        kernel,
        grid=(bsz, DIFF_HEADS, nblk),
        in_specs=[
            pl.BlockSpec((hw, blk), lambda b, h, i: (h, b * nblk + i)),
            pl.BlockSpec((seq, hw), lambda b, h, i: (b, h)),
            pl.BlockSpec((hw, seq), lambda b, h, i: (h, b)),
            pl.BlockSpec((1, 2, blk, blk), lambda b, h, i: (h, 0, 0, 0)),
            pl.BlockSpec((1, 1, blk), lambda b, h, i: (h, 0, 0)),
            pl.BlockSpec((4, DIFF_DH), lambda b, h, i: (0, 0)),
            pl.BlockSpec((1, hw, blk), lambda b, h, i: (h, 0, 0)),
        ],
        out_specs=pl.BlockSpec((blk, hw), lambda b, h, i: (b * nblk + i, h)),
        out_shape=jax.ShapeDtypeStruct((t, DIFF_HEADS * hw), BF16),
        scratch_shapes=[pltpu.VMEM((2, 1, blk), F32), pltpu.VMEM((2, 1, blk), F32),
                        pltpu.VMEM((2, hw, blk), F32), pltpu.VMEM((2, 2, blk, blk), BF16),
                        pltpu.VMEM((2, 2, 1, blk), F32)],
        compiler_params=_params("arbitrary", "arbitrary", "arbitrary"),
        name="diff_attention",
    )(qt, kd, vt, nb, slope_rows, lam_params, gn)


def _conv_module_kernel(ca_ref, cb_ref, cah_ref, cbh_ref, w_ref, b_ref, g_ref, beta_ref, o_ref, xe_ref,
                        xs_ref, *, rows, sub):
    first = pl.program_id(1) == 0
    halo = cah_ref[...].astype(F32) * jax.nn.sigmoid(cbh_ref[...].astype(F32))
    xe_ref[0:CONV_HALO, :] = jnp.where(first, 0.0, halo)
    xe_ref[CONV_HALO:CONV_HALO + rows, :] = ca_ref[...].astype(F32) * jax.nn.sigmoid(cb_ref[...].astype(F32))

    base = CONV_HALO - (CONV_K - 1)
    span = xs_ref.shape[1]
    for s in range(1, 8):
        xs_ref[s - 1] = xe_ref[s:s + span, :]

    for r0 in range(0, rows, sub):
        acc = jnp.broadcast_to(b_ref[...], (sub, CONV_CH))
        for k in range(CONV_K):
            a, s = divmod(base + k, 8)
            src = xe_ref if s == 0 else xs_ref.at[s - 1]
            acc = acc + jnp.tile(w_ref[k], (sub // 8, 1)) * src[r0 + 8 * a:r0 + 8 * a + sub, :]
        mu = jnp.mean(acc, axis=-1, keepdims=True)
        xc = acc - mu
        var = jnp.mean(xc * xc, axis=-1, keepdims=True)
        y = xc * lax.rsqrt(var + EPS) * g_ref[...] + beta_ref[...]
        o_ref[r0:r0 + sub, :] = (y * jax.nn.sigmoid(y)).astype(o_ref.dtype)


def _conv_module(z, w, b, ln_g, ln_b, bsz, seq):
    t = z.shape[0]
    rows = _tile(seq, 256)
    nblk = seq // rows
    hpb = rows // CONV_HALO
    main = lambda col: pl.BlockSpec((rows, CONV_CH), lambda bb, i: (bb * nblk + i, col // CONV_CH))
    halo = lambda col: pl.BlockSpec(
        (CONV_HALO, CONV_CH), lambda bb, i: (jnp.maximum((bb * nblk + i) * hpb - 1, 0), col // CONV_CH))
    vec = pl.BlockSpec((1, CONV_CH), lambda bb, i: (0, 0))
    kernel = functools.partial(_conv_module_kernel, rows=rows, sub=32)
    return pl.pallas_call(
        kernel,
        grid=(bsz, nblk),
        in_specs=[main(COL_CA), main(COL_CB), halo(COL_CA), halo(COL_CB),
                  pl.BlockSpec((CONV_K, 8, CONV_CH), lambda bb, i: (0, 0, 0)), vec, vec, vec],
        out_specs=pl.BlockSpec((rows, CONV_CH), lambda bb, i: (bb * nblk + i, 0)),
        out_shape=jax.ShapeDtypeStruct((t, CONV_CH), BF16),
        scratch_shapes=[pltpu.VMEM((CONV_HALO + rows, CONV_CH), F32),
                        pltpu.VMEM((7, CONV_HALO + rows - 8, CONV_CH), F32)],
        compiler_params=_params("arbitrary", "arbitrary"),
        name="conv_module",
    )(z, z, z, z, jnp.broadcast_to(w[:, None, :], (CONV_K, 8, CONV_CH)), b.reshape(1, CONV_CH),
      ln_g.reshape(1, CONV_CH), ln_b.reshape(1, CONV_CH))


def _merge_kernel(h_ref, rinv_ref, wg0_ref, wg1_ref, wg2_ref, bg0_ref, bg1_ref, bg2_ref, ya_ref, yb_ref,
                  yc_ref, wbr_ref, o_ref):
    h = h_ref[...]
    rinv = jnp.tile(rinv_ref[...], (1, o_ref.shape[1] // LANES))
    merged = None
    for j, (wg_ref, bg_ref, y_ref) in enumerate(((wg0_ref, bg0_ref, ya_ref), (wg1_ref, bg1_ref, yb_ref),
                                                 (wg2_ref, bg2_ref, yc_ref))):
        gate = jax.nn.sigmoid(jnp.dot(h, wg_ref[...], preferred_element_type=F32) * rinv + bg_ref[...])
        term = gate * jnp.dot(y_ref[...], wbr_ref[j], preferred_element_type=F32)
        merged = term if merged is None else merged + term
    o_ref[...] = merged.astype(o_ref.dtype)


def _gated_merge(h, rinv, w_gate, b_gate, ya, yb, yc, w_br, layer, casts=()):
    t, d = h.shape
    tm, tn = _tile(t, 512), _tile(d, 512)
    nd = d // tn
    ysp = pl.BlockSpec((tm, BRANCH_WIDTH), lambda n, m: (m, 0))
    wsp = [pl.BlockSpec((None, d, tn), lambda n, m, j=j: (layer, 0, j * nd + n)) for j in range(N_BRANCH)]
    bsp = [pl.BlockSpec((1, tn), lambda n, m, j=j: (0, j * nd + n)) for j in range(N_BRANCH)]
    (out,), converted = _pallas_with_casts(
        _merge_kernel, (h, rinv, w_gate, w_gate, w_gate, b_gate, b_gate, b_gate, ya, yb, yc, w_br),
        grid=(nd, t // tm),
        in_specs=[pl.BlockSpec((tm, d), lambda n, m: (m, 0)), pl.BlockSpec((tm, LANES), lambda n, m: (m, 0)),
                  *wsp, *bsp, ysp, ysp, ysp,
                  pl.BlockSpec((None, N_BRANCH, BRANCH_WIDTH, tn), lambda n, m: (layer, 0, 0, n))],
        out_specs=[pl.BlockSpec((tm, tn), lambda n, m: (m, n))],
        out_shape=[jax.ShapeDtypeStruct((t, d), BF16)],
        name="gated_merge", casts=casts)
    return out, converted


def _ffn_in_kernel(h_ref, rinv_ref, wg_ref, wu_ref, cw_ref, cb_ref, o_ref, fg_ref, *, tm, tn, rsub, csub,
                   tiles_per_seq):
    seq_start = pl.program_id(1) % tiles_per_seq == 0

    @pl.when(seq_start)
    def _():
        fg_ref[0:FFN_HALO, :] = jnp.zeros((FFN_HALO, tn), F32)

    @pl.when(jnp.logical_not(seq_start))
    def _():
        fg_ref[0:FFN_HALO, :] = fg_ref[tm:tm + FFN_HALO, :]

    def project(r0, c0):
        h = h_ref[r0:r0 + rsub, :]
        rinv = rinv_ref[r0:r0 + rsub, :]
        fg = _row_scale(jnp.dot(h, wg_ref[:, c0:c0 + csub], preferred_element_type=F32), rinv)
        fu = _row_scale(jnp.dot(h, wu_ref[:, c0:c0 + csub], preferred_element_type=F32), rinv)
        fg_ref[FFN_HALO + r0:FFN_HALO + r0 + rsub, c0:c0 + csub] = fg
        return fg, fu

    def finish(r0, c0, fg, fu):
        cols = slice(c0, c0 + csub)
        conv = (cw_ref[0:1, cols] * fg_ref[FFN_HALO - 2 + r0:FFN_HALO - 2 + r0 + rsub, cols]
                + cw_ref[1:2, cols] * fg_ref[FFN_HALO - 1 + r0:FFN_HALO - 1 + r0 + rsub, cols]
                + cw_ref[2:3, cols] * fg + cb_ref[:, cols])
        gelu = 0.5 * conv * (1.0 + lax.erf(conv * (2.0 ** -0.5)))
        o_ref[r0:r0 + rsub, cols] = (gelu * fu).astype(o_ref.dtype)

    pending = None
    for r0 in range(0, tm, rsub):
        for c0 in range(0, tn, csub):
            projected = project(r0, c0)
            if pending is not None:
                finish(*pending)
            pending = (r0, c0, *projected)
    finish(*pending)


def _ffn_in(h, rinv, w_in, layer, conv_w, conv_b, seq, casts=()):
    t, d = h.shape
    f = w_in.shape[2] // 2
    tm, tn = _tile(seq, 1024), _tile(f, 512)
    rsub, csub = _tile(tm, FFN_ROW_SUB), _tile(tn, FFN_COL_SUB)
    nf = f // tn
    kernel = functools.partial(_ffn_in_kernel, tm=tm, tn=tn, rsub=rsub, csub=csub, tiles_per_seq=seq // tm)
    (out,), converted = _pallas_with_casts(
        kernel, (h, rinv, w_in, w_in, conv_w, conv_b.reshape(1, f)),
        grid=(nf, t // tm),
        in_specs=[pl.BlockSpec((tm, d), lambda n, m: (m, 0)),
                  pl.BlockSpec((tm, LANES), lambda n, m: (m, 0)),
                  pl.BlockSpec((None, d, tn), lambda n, m: (layer, 0, n)),
                  pl.BlockSpec((None, d, tn), lambda n, m: (layer, 0, nf + n)),
                  pl.BlockSpec((FFN_CONV_K, tn), lambda n, m: (0, n)),
                  pl.BlockSpec((1, tn), lambda n, m: (0, n))],
        out_specs=[pl.BlockSpec((tm, tn), lambda n, m: (m, n))],
        out_shape=[jax.ShapeDtypeStruct((t, f), BF16)],
        scratch_shapes=[pltpu.VMEM((FFN_HALO + tm, tn), F32)],
        name="ffn_in", casts=casts)
    return out, converted


def kernel(x, norm_mix, w_in, w_gate, b_gate, diff_q_norm, diff_k_norm, diff_lambda, diff_out_norm,
           ret_out_norm, conv_w, conv_b, conv_ln_g, conv_ln_b, w_br, w_o, norm_ffn, w_ffn_in,
           ffn_conv_w, ffn_conv_b, w_ffn_out):
    bsz, seq, d = x.shape
    depth = w_in.shape[0]
    t = bsz * seq
    x = x.reshape(t, d)
    wi, wg, wb, wo = (w[0:1].astype(BF16) for w in (w_in, w_gate, w_br, w_o))
    h, rinv = _norm_prep(x, norm_mix[0])
    for l in range(depth):
        lam_init = 0.8 - 0.6 * math.exp(-0.3 * l)
        nxt = l + 1 < depth
        z, (wfi,) = _matmul(h, rinv, wi, 0, BF16, casts=((w_ffn_in, l),))
        ya = _retention(z, ret_out_norm[l], bsz, seq)
        qt, kd, vt = _qk_norm(z, diff_q_norm[l], diff_k_norm[l])
        yb = _diff_attention(qt, kd, vt, diff_lambda[l], diff_out_norm[l], lam_init, bsz, seq)
        yc = _conv_module(z, conv_w[l], conv_b[l], conv_ln_g[l], conv_ln_b[l], bsz, seq)
        merged, (wfo,) = _gated_merge(h, rinv, wg, b_gate[l].reshape(1, N_BRANCH * d), ya, yb, yc, wb, 0,
                                      casts=((w_ffn_out, l),))
        (x, h, rinv), _ = _matmul_residual(merged, wo, 0, x, norm_ffn[l])
        act, nxt_in = _ffn_in(h, rinv, wfi, 0, ffn_conv_w[l], ffn_conv_b[l], seq,
                              casts=((w_in, l + 1), (w_o, l + 1)) if nxt else ())
        if nxt:
            (x, h, rinv), (wg, wb) = _matmul_residual(act, wfo, 0, x, norm_mix[l + 1],
                                                      casts=((w_gate, l + 1), (w_br, l + 1)))
            wi, wo = nxt_in
        else:
            x = _matmul_residual(act, wfo, 0, x)
    return x.reshape(bsz, seq, d)
```

```python
import functools
import math

import jax
import jax.numpy as jnp
from jax import lax
from jax.experimental import pallas as pl
from jax.experimental.pallas import tpu as pltpu

EPS = 1e-6
NEG_INF = -1e30
CHUNK = 64
LOG2E = 1.4426950408889634

RET_HEADS = 4
RET_DK = 128
RET_DV = 256
DIFF_HEADS = 4
DIFF_DH = 128
CONV_CH = 1024
CONV_K = 31
FFN_CONV_K = 3
BRANCH_WIDTH = 1024
N_BRANCH = 3

COL_RQ = 0
COL_RK = COL_RQ + RET_HEADS * RET_DK
COL_RV = COL_RK + RET_HEADS * RET_DK
COL_RG = COL_RV + RET_HEADS * RET_DV
COL_DQ = COL_RG + RET_HEADS * RET_DV
COL_DK = COL_DQ + DIFF_HEADS * 2 * DIFF_DH
COL_DV = COL_DK + DIFF_HEADS * 2 * DIFF_DH
COL_CA = COL_DV + DIFF_HEADS * 2 * DIFF_DH
COL_CB = COL_CA + CONV_CH

LANES = 128
CAST_ROWS = 16
VMEM_LIMIT_BYTES = 56 * 1024 * 1024
CONV_HALO = 32
FFN_HALO = 8
FFN_ROW_SUB = 512
FFN_COL_SUB = 512

F32 = jnp.float32
BF16 = jnp.bfloat16


def _params(*sem):
    return pltpu.CompilerParams(dimension_semantics=sem, vmem_limit_bytes=VMEM_LIMIT_BYTES)


def _tile(n, pref):
    t = min(n, pref)
    assert n % t == 0, (n, pref)
    return t


def _pallas_with_casts(kernel, args, *, grid, in_specs, out_specs, out_shape, name, scratch_shapes=(), casts=()):
    out_specs, out_shape = list(out_specs), list(out_shape)
    n_in, n_out, n_cast = len(in_specs), len(out_shape), len(casts)
    inner = grid[1]
    steps = grid[0] * grid[1]
    views, slab_in, slab_out = [], [], []
    for w, layer in casts:
        v = w.reshape(w.shape[0], -1, w.shape[-1])
        rows, cols = v.shape[1:]
        slab = CAST_ROWS * pl.cdiv(rows, CAST_ROWS * steps)
        assert rows % slab == 0, (w.shape, grid)
        last = rows // slab - 1
        views.append(v)
        slab_in.append(pl.BlockSpec((None, slab, cols), lambda i, j, layer=layer, last=last:
                                    (layer, jnp.minimum(i * inner + j, last), 0)))
        slab_out.append(pl.BlockSpec((slab, cols), lambda i, j, last=last: (jnp.minimum(i * inner + j, last), 0)))

    def body(*refs):
        ins, refs = refs[:n_in], refs[n_in:]
        cast_in, refs = refs[:n_cast], refs[n_cast:]
        outs, refs = refs[:n_out], refs[n_out:]
        cast_out, scratch = refs[:n_cast], refs[n_cast:]
        kernel(*ins, *outs, *scratch)
        for src, dst in zip(cast_in, cast_out):
            dst[...] = src[...].astype(dst.dtype)

    res = pl.pallas_call(
        body,
        grid=grid,
        in_specs=[*in_specs, *slab_in],
        out_specs=[*out_specs, *slab_out],
        out_shape=[*out_shape, *[jax.ShapeDtypeStruct(v.shape[1:], BF16) for v in views]],
        scratch_shapes=list(scratch_shapes),
        compiler_params=_params("arbitrary", "arbitrary"),
        name=name,
    )(*args, *views)
    converted = [c.reshape((1,) + w.shape[1:]) for c, (w, _) in zip(res[n_out:], casts)]
    return res[:n_out], converted


def _lane_fold(v):
    out = v[:, 0:LANES]
    for c in range(1, v.shape[1] // LANES):
        out = out + v[:, c * LANES:(c + 1) * LANES]
    return out


def _rinv_from_folded(ss, d):
    total = jnp.sum(ss, axis=-1, keepdims=True)
    return jnp.broadcast_to(lax.rsqrt(total * (1.0 / d) + EPS), ss.shape)


def _row_scale(acc, rinv):
    return acc * jnp.tile(rinv, (1, acc.shape[1] // LANES))


def _norm_prep_kernel(x_ref, g_ref, xg_ref, rinv_ref):
    x = x_ref[...]
    xg_ref[...] = (x * g_ref[...]).astype(xg_ref.dtype)
    rinv_ref[...] = _rinv_from_folded(_lane_fold(x * x), x.shape[1])


def _norm_prep(x, g):
    t, d = x.shape
    tr = _tile(t, 256)
    return pl.pallas_call(
        _norm_prep_kernel,
        grid=(t // tr,),
        in_specs=[pl.BlockSpec((tr, d), lambda i: (i, 0)),
                  pl.BlockSpec((1, d), lambda i: (0, 0))],
        out_specs=[pl.BlockSpec((tr, d), lambda i: (i, 0)), pl.BlockSpec((tr, LANES), lambda i: (i, 0))],
        out_shape=[jax.ShapeDtypeStruct((t, d), BF16), jax.ShapeDtypeStruct((t, LANES), F32)],
        compiler_params=_params("arbitrary"),
        name="norm_prep",
    )(x, g.reshape(1, d))


def _mm_kernel(a_ref, rinv_ref, b_ref, o_ref):
    acc = jnp.dot(a_ref[...], b_ref[...], preferred_element_type=F32)
    o_ref[...] = _row_scale(acc, rinv_ref[...]).astype(o_ref.dtype)


def _matmul(a, rinv, b, layer, out_dtype, casts=()):
    m, k = a.shape
    n = b.shape[2]
    tm, tn = _tile(m, 1024), _tile(n, 1024)
    (out,), converted = _pallas_with_casts(
        _mm_kernel, (a, rinv, b),
        grid=(m // tm, n // tn),
        in_specs=[pl.BlockSpec((tm, k), lambda i, j: (i, 0)),
                  pl.BlockSpec((tm, LANES), lambda i, j: (i, 0)),
                  pl.BlockSpec((None, k, tn), lambda i, j: (layer, 0, j))],
        out_specs=[pl.BlockSpec((tm, tn), lambda i, j: (i, j))],
        out_shape=[jax.ShapeDtypeStruct((m, n), out_dtype)],
        name="matmul", casts=casts)
    return out, converted


def _mm_res_kernel(a_ref, b_ref, r_ref, o_ref):
    o_ref[...] = r_ref[...] + jnp.dot(a_ref[...], b_ref[...], preferred_element_type=F32)


def _mm_res_norm_kernel(a_ref, b_ref, r_ref, g_ref, o_ref, xg_ref, rinv_ref, ss_ref, *, d, rsub):
    @pl.when(pl.program_id(1) == 0)
    def _():
        ss_ref[...] = jnp.zeros_like(ss_ref)

    def project(r0):
        return r_ref[r0:r0 + rsub, :] + jnp.dot(a_ref[r0:r0 + rsub, :], b_ref[...], preferred_element_type=F32)

    def finish(r0, x):
        rows = slice(r0, r0 + rsub)
        o_ref[rows, :] = x
        xg_ref[rows, :] = (x * g_ref[...]).astype(xg_ref.dtype)
        ss = _lane_fold(x * x) + ss_ref[rows, :]
        ss_ref[rows, :] = ss
        rinv_ref[rows, :] = _rinv_from_folded(ss, d)

    pending = None
    for r0 in range(0, a_ref.shape[0], rsub):
        x = project(r0)
        if pending is not None:
            finish(*pending)
        pending = (r0, x)
    finish(*pending)


def _matmul_residual(a, b, layer, res, g_next=None, casts=()):
    m, k = a.shape
    n = b.shape[2]
    tm = _tile(m, 1024)
    tn = _tile(n, (2 * 1024 * 1024) // k)
    in_specs = [pl.BlockSpec((tm, k), lambda i, j: (i, 0)),
                pl.BlockSpec((None, k, tn), lambda i, j: (layer, 0, j)),
                pl.BlockSpec((tm, tn), lambda i, j: (i, j))]
    tile = pl.BlockSpec((tm, tn), lambda i, j: (i, j))
    if g_next is None:
        return pl.pallas_call(
            _mm_res_kernel,
            grid=(m // tm, n // tn),
            in_specs=in_specs,
            out_specs=tile,
            out_shape=jax.ShapeDtypeStruct((m, n), F32),
            compiler_params=_params("arbitrary", "arbitrary"),
            name="matmul_residual",
        )(a, b, res)
    return _pallas_with_casts(
        functools.partial(_mm_res_norm_kernel, d=n, rsub=_tile(tm, 512)), (a, b, res, g_next.reshape(1, n)),
        grid=(m // tm, n // tn),
        in_specs=in_specs + [pl.BlockSpec((1, tn), lambda i, j: (0, j))],
        out_specs=[tile, tile, pl.BlockSpec((tm, LANES), lambda i, j: (i, 0))],
        out_shape=[jax.ShapeDtypeStruct((m, n), F32), jax.ShapeDtypeStruct((m, n), BF16),
                   jax.ShapeDtypeStruct((m, LANES), F32)],
        scratch_shapes=[pltpu.VMEM((tm, LANES), F32)],
        name="matmul_residual_norm", casts=casts)


def _retention_kernel(q_ref, k_ref, v_ref, g_ref, mask_ref, qdec_ref, kdec_ref, cdec_ref, gn_ref,
                      o_ref, state_ref):
    @pl.when(pl.program_id(1) == 0)
    def _():
        state_ref[...] = jnp.zeros_like(state_ref)

    heads = range(RET_HEADS)
    q = [q_ref[:, h * RET_DK:(h + 1) * RET_DK] for h in heads]
    k = [k_ref[:, h * RET_DK:(h + 1) * RET_DK] for h in heads]
    v = [v_ref[:, h * RET_DV:(h + 1) * RET_DV] for h in heads]
    scores = [lax.dot_general(q[h], k[h], (((1,), (1,)), ((), ())), preferred_element_type=F32) * mask_ref[h]
              for h in heads]
    state = [state_ref[h] for h in heads]
    cross = [jnp.dot((q[h].astype(F32) * qdec_ref[h]).astype(BF16), state[h].astype(BF16),
                     preferred_element_type=F32) for h in heads]
    inner = [jnp.dot(scores[h].astype(BF16), v[h], preferred_element_type=F32) for h in heads]
    for h in heads:
        kd = (k[h].astype(F32) * kdec_ref[h]).astype(BF16)
        kv = lax.dot_general(kd, v[h], (((0,), (0,)), ((), ())), preferred_element_type=F32)
        state_ref[h] = state[h] * cdec_ref[h] + kv
    for h in heads:
        out = inner[h] + cross[h]
        ms = jnp.mean(out * out, axis=-1, keepdims=True)
        y = out * lax.rsqrt(ms + EPS) * gn_ref[h]
        g = g_ref[:, h * RET_DV:(h + 1) * RET_DV].astype(F32)
        o_ref[:, h * RET_DV:(h + 1) * RET_DV] = (g * jax.nn.sigmoid(g) * y).astype(o_ref.dtype)


def _retention_tables(blk):
    lg = jnp.log(1.0 - 2.0 ** (-5.0 - jnp.arange(RET_HEADS, dtype=F32)))
    pos = jnp.arange(blk, dtype=F32)
    dist = jnp.abs(pos[:, None] - pos[None, :])
    allowed = (jnp.arange(blk)[None, :] // CHUNK) <= (jnp.arange(blk)[:, None] // CHUNK)
    kscale = RET_DK ** -0.5
    mask = jnp.where(allowed[None], jnp.exp(dist[None] * lg[:, None, None]), 0.0) * kscale
    qdec = jnp.exp((pos[None, :] + 1.0) * lg[:, None])
    kdec = jnp.exp((blk - 1.0 - pos[None, :]) * lg[:, None]) * kscale
    cdec = jnp.exp(blk * lg)
    qdec = jnp.broadcast_to(qdec[:, :, None], (RET_HEADS, blk, RET_DK))
    kdec = jnp.broadcast_to(kdec[:, :, None], (RET_HEADS, blk, RET_DK))
    cdec = jnp.broadcast_to(cdec[:, None, None], (RET_HEADS, 1, RET_DV))
    return mask, qdec, kdec, cdec


def _retention(z, out_norm, bsz, seq):
    t = z.shape[0]
    blk = _tile(seq, 256)
    nblk = seq // blk
    mask, qdec, kdec, cdec = _retention_tables(blk)
    qk_w, v_w = RET_HEADS * RET_DK, RET_HEADS * RET_DV
    whole = lambda shape: pl.BlockSpec(shape, lambda b, i: (0,) * len(shape))
    return pl.pallas_call(
        _retention_kernel,
        grid=(bsz, nblk),
        in_specs=[
            pl.BlockSpec((blk, qk_w), lambda b, i: (b * nblk + i, COL_RQ // qk_w)),
            pl.BlockSpec((blk, qk_w), lambda b, i: (b * nblk + i, COL_RK // qk_w)),
            pl.BlockSpec((blk, v_w), lambda b, i: (b * nblk + i, COL_RV // v_w)),
            pl.BlockSpec((blk, v_w), lambda b, i: (b * nblk + i, COL_RG // v_w)),
            whole((RET_HEADS, blk, blk)),
            whole((RET_HEADS, blk, RET_DK)),
            whole((RET_HEADS, blk, RET_DK)),
            whole((RET_HEADS, 1, RET_DV)),
            whole((RET_HEADS, 1, RET_DV)),
        ],
        out_specs=pl.BlockSpec((blk, v_w), lambda b, i: (b * nblk + i, 0)),
        out_shape=jax.ShapeDtypeStruct((t, v_w), BF16),
        scratch_shapes=[pltpu.VMEM((RET_HEADS, RET_DK, RET_DV), F32)],
        compiler_params=_params("arbitrary", "arbitrary"),
        name="retention",
    )(z, z, z, z, mask, qdec, kdec, cdec, out_norm.reshape(RET_HEADS, 1, RET_DV))


def _qk_norm_kernel(q_ref, k_ref, v_ref, gq_ref, gk_ref, qt_ref, ko_ref, vt_ref):
    gq = gq_ref[...] * (DIFF_DH ** -0.5 * LOG2E)
    gk = gk_ref[...]
    for c in range(q_ref.shape[1] // DIFF_DH):
        cols = slice(c * DIFF_DH, (c + 1) * DIFF_DH)
        q = q_ref[:, cols].astype(F32)
        q = q * lax.rsqrt(jnp.mean(q * q, axis=-1, keepdims=True) + EPS) * gq
        qt_ref[cols, :] = q.T.astype(qt_ref.dtype)
        k = k_ref[:, cols].astype(F32)
        k = k * lax.rsqrt(jnp.mean(k * k, axis=-1, keepdims=True) + EPS) * gk
        ko_ref[:, cols] = k.astype(ko_ref.dtype)
        vt_ref[cols, :] = v_ref[:, cols].astype(F32).T.astype(vt_ref.dtype)


def _qk_norm(z, gq, gk):
    t = z.shape[0]
    w = DIFF_HEADS * 2 * DIFF_DH
    tr = _tile(t, 512)
    spec = lambda col: pl.BlockSpec((tr, w), lambda i: (i, col // w))
    vec = pl.BlockSpec((1, DIFF_DH), lambda i: (0, 0))
    rows = pl.BlockSpec((tr, w), lambda i: (i, 0))
    cols = pl.BlockSpec((w, tr), lambda i: (0, i))
    return pl.pallas_call(
        _qk_norm_kernel,
        grid=(t // tr,),
        in_specs=[spec(COL_DQ), spec(COL_DK), spec(COL_DV), vec, vec],
        out_specs=[cols, rows, cols],
        out_shape=[jax.ShapeDtypeStruct((w, t), BF16), jax.ShapeDtypeStruct((t, w), BF16),
                   jax.ShapeDtypeStruct((w, t), BF16)],
        compiler_params=_params("arbitrary"),
        name="qk_norm",
    )(z, z, z, gq.reshape(1, DIFF_DH), gk.reshape(1, DIFF_DH))


def _diff_attn_kernel(qt_ref, k_ref, vt_ref, nb_ref, slope_ref, lam_ref, gn_ref, o_ref,
                      m_sc, l_sc, acc_sc, p_sc, a_sc, *, blk, lam_init):
    i = pl.program_id(2)
    m_sc[...] = jnp.full_like(m_sc, NEG_INF)
    l_sc[...] = jnp.zeros_like(l_sc)
    acc_sc[...] = jnp.zeros_like(acc_sc)

    def scores(j):
        ks = k_ref[pl.ds(pl.multiple_of(j * blk, blk), blk), :]
        bias = nb_ref[0, (j == i).astype(jnp.int32)]
        return [jnp.dot(ks[:, m * DIFF_DH:(m + 1) * DIFF_DH], qt_ref[m * DIFF_DH:(m + 1) * DIFF_DH, :],
                        preferred_element_type=F32) + bias for m in range(2)]

    def softmax_update(j, slot, s1s):
        soff = slope_ref[0] * ((i - j) * blk).astype(F32)
        for m in range(2):
            m_old = m_sc[m]
            m_new = jnp.maximum(m_old, jnp.max(s1s[m], axis=0, keepdims=True) - soff)
            alpha = jnp.exp2(m_old - m_new)
            p = jnp.exp2(s1s[m] - (m_new + soff))
            l_sc[m] = alpha * l_sc[m] + jnp.sum(p, axis=0, keepdims=True)
            m_sc[m] = m_new
            p_sc[slot, m] = p.astype(BF16)
            a_sc[slot, m] = alpha

    def add_values(j, slot):
        vts = vt_ref[:, pl.ds(pl.multiple_of(j * blk, blk), blk)]
        for m in range(2):
            acc_sc[m] = a_sc[slot, m] * acc_sc[m] + jnp.dot(vts, p_sc[slot, m], preferred_element_type=F32)

    def step(j, slot):
        s1s = scores(j)
        add_values(j - 1, 1 - slot)
        softmax_update(j, slot, s1s)

    softmax_update(0, 0, scores(0))

    def pair(t, carry):
        step(2 * t + 1, 1)
        step(2 * t + 2, 0)
        return carry

    lax.fori_loop(0, i // 2, pair, 0)

    @pl.when(i % 2 == 1)
    def _():
        step(i, 1)
        add_values(i, 1)

    @pl.when(i % 2 == 0)
    def _():
        add_values(i, 0)

    lv = lam_ref[...]
    lam = (jnp.exp(jnp.sum(lv[0:1] * lv[1:2], axis=-1, keepdims=True))
           - jnp.exp(jnp.sum(lv[2:3] * lv[3:4], axis=-1, keepdims=True)) + lam_init)
    out = acc_sc[0] / l_sc[0] - lam * (acc_sc[1] / l_sc[1])
    ms = jnp.mean(out * out, axis=0, keepdims=True)
    y = out * lax.rsqrt(ms + EPS) * gn_ref[0] * (1.0 - lam_init)
    o_ref[...] = y.T.astype(o_ref.dtype)


def _diff_attention(qt, kd, vt, lam_params, out_norm, lam_init, bsz, seq):
    t = kd.shape[0]
    hw = 2 * DIFF_DH
    blk = _tile(seq, 512)
    nblk = seq // blk
    slopes = 2.0 ** (-8.0 * (jnp.arange(DIFF_HEADS, dtype=F32) + 1.0) / DIFF_HEADS) * LOG2E
    r = jnp.arange(blk)[None, :]
    c = jnp.arange(blk)[:, None]
    past = -slopes[:, None, None] * (r - c).astype(F32)[None]
    allowed = (c // CHUNK) <= (r // CHUNK)
    diag = jnp.where(allowed[None], -slopes[:, None, None] * jnp.abs(r - c).astype(F32)[None], NEG_INF)
    nb = jnp.stack([past, diag], axis=1)
    slope_rows = jnp.broadcast_to(slopes[:, None, None], (DIFF_HEADS, 1, blk))
    gn = jnp.broadcast_to(out_norm.reshape(DIFF_HEADS, hw, 1), (DIFF_HEADS, hw, blk))
    kernel = functools.partial(_diff_attn_kernel, blk=blk, lam_init=lam_init)
    return pl.pallas_call(
        kernel,
        grid=(bsz, DIFF_HEADS, nblk),
        in_specs=[
            pl.BlockSpec((hw, blk), lambda b, h, i: (h, b * nblk + i)),
            pl.BlockSpec((seq, hw), lambda b, h, i: (b, h)),
            pl.BlockSpec((hw, seq), lambda b, h, i: (h, b)),
            pl.BlockSpec((1, 2, blk, blk), lambda b, h, i: (h, 0, 0, 0)),
            pl.BlockSpec((1, 1, blk), lambda b, h, i: (h, 0, 0)),
            pl.BlockSpec((4, DIFF_DH), lambda b, h, i: (0, 0)),
            pl.BlockSpec((1, hw, blk), lambda b, h, i: (h, 0, 0)),
        ],
        out_specs=pl.BlockSpec((blk, hw), lambda b, h, i: (b * nblk + i, h)),
        out_shape=jax.ShapeDtypeStruct((t, DIFF_HEADS * hw), BF16),
        scratch_shapes=[pltpu.VMEM((2, 1, blk), F32), pltpu.VMEM((2, 1, blk), F32),
                        pltpu.VMEM((2, hw, blk), F32), pltpu.VMEM((2, 2, blk, blk), BF16),
                        pltpu.VMEM((2, 2, 1, blk), F32)],
        compiler_params=_params("arbitrary", "arbitrary", "arbitrary"),
        name="diff_attention",
    )(qt, kd, vt, nb, slope_rows, lam_params, gn)


def _conv_module_kernel(ca_ref, cb_ref, cah_ref, cbh_ref, w_ref, b_ref, g_ref, beta_ref, o_ref, xe_ref,
                        xs_ref, *, rows, sub):
    first = pl.program_id(1) == 0
    halo = cah_ref[...].astype(F32) * jax.nn.sigmoid(cbh_ref[...].astype(F32))
    xe_ref[0:CONV_HALO, :] = jnp.where(first, 0.0, halo)
    xe_ref[CONV_HALO:CONV_HALO + rows, :] = ca_ref[...].astype(F32) * jax.nn.sigmoid(cb_ref[...].astype(F32))

    base = CONV_HALO - (CONV_K - 1)
    span = xs_ref.shape[1]
    for s in range(1, 8):
        xs_ref[s - 1] = xe_ref[s:s + span, :]

    for r0 in range(0, rows, sub):
        acc = jnp.broadcast_to(b_ref[...], (sub, CONV_CH))
        for k in range(CONV_K):
            a, s = divmod(base + k, 8)
            src = xe_ref if s == 0 else xs_ref.at[s - 1]
            acc = acc + jnp.tile(w_ref[k], (sub // 8, 1)) * src[r0 + 8 * a:r0 + 8 * a + sub, :]
        mu = jnp.mean(acc, axis=-1, keepdims=True)
        xc = acc - mu
        var = jnp.mean(xc * xc, axis=-1, keepdims=True)
        y = xc * lax.rsqrt(var + EPS) * g_ref[...] + beta_ref[...]
        o_ref[r0:r0 + sub, :] = (y * jax.nn.sigmoid(y)).astype(o_ref.dtype)


def _conv_module(z, w, b, ln_g, ln_b, bsz, seq):
    t = z.shape[0]
    rows = _tile(seq, 256)
    nblk = seq // rows
    hpb = rows // CONV_HALO
    main = lambda col: pl.BlockSpec((rows, CONV_CH), lambda bb, i: (bb * nblk + i, col // CONV_CH))
    halo = lambda col: pl.BlockSpec(
        (CONV_HALO, CONV_CH), lambda bb, i: (jnp.maximum((bb * nblk + i) * hpb - 1, 0), col // CONV_CH))
    vec = pl.BlockSpec((1, CONV_CH), lambda bb, i: (0, 0))
    kernel = functools.partial(_conv_module_kernel, rows=rows, sub=32)
    return pl.pallas_call(
        kernel,
        grid=(bsz, nblk),
        in_specs=[main(COL_CA), main(COL_CB), halo(COL_CA), halo(COL_CB),
                  pl.BlockSpec((CONV_K, 8, CONV_CH), lambda bb, i: (0, 0, 0)), vec, vec, vec],
        out_specs=pl.BlockSpec((rows, CONV_CH), lambda bb, i: (bb * nblk + i, 0)),
        out_shape=jax.ShapeDtypeStruct((t, CONV_CH), BF16),
        scratch_shapes=[pltpu.VMEM((CONV_HALO + rows, CONV_CH), F32),
                        pltpu.VMEM((7, CONV_HALO + rows - 8, CONV_CH), F32)],
        compiler_params=_params("arbitrary", "arbitrary"),
        name="conv_module",
    )(z, z, z, z, jnp.broadcast_to(w[:, None, :], (CONV_K, 8, CONV_CH)), b.reshape(1, CONV_CH),
      ln_g.reshape(1, CONV_CH), ln_b.reshape(1, CONV_CH))


def _merge_kernel(h_ref, rinv_ref, wg0_ref, wg1_ref, wg2_ref, bg0_ref, bg1_ref, bg2_ref, ya_ref, yb_ref,
                  yc_ref, wbr_ref, o_ref):
    h = h_ref[...]
    rinv = jnp.tile(rinv_ref[...], (1, o_ref.shape[1] // LANES))
    merged = None
    for j, (wg_ref, bg_ref, y_ref) in enumerate(((wg0_ref, bg0_ref, ya_ref), (wg1_ref, bg1_ref, yb_ref),
                                                 (wg2_ref, bg2_ref, yc_ref))):
        gate = jax.nn.sigmoid(jnp.dot(h, wg_ref[...], preferred_element_type=F32) * rinv + bg_ref[...])
        term = gate * jnp.dot(y_ref[...], wbr_ref[j], preferred_element_type=F32)
        merged = term if merged is None else merged + term
    o_ref[...] = merged.astype(o_ref.dtype)


def _gated_merge(h, rinv, w_gate, b_gate, ya, yb, yc, w_br, layer, casts=()):
    t, d = h.shape
    tm, tn = _tile(t, 512), _tile(d, 512)
    nd = d // tn
    ysp = pl.BlockSpec((tm, BRANCH_WIDTH), lambda n, m: (m, 0))
    wsp = [pl.BlockSpec((None, d, tn), lambda n, m, j=j: (layer, 0, j * nd + n)) for j in range(N_BRANCH)]
    bsp = [pl.BlockSpec((1, tn), lambda n, m, j=j: (0, j * nd + n)) for j in range(N_BRANCH)]
    (out,), converted = _pallas_with_casts(
        _merge_kernel, (h, rinv, w_gate, w_gate, w_gate, b_gate, b_gate, b_gate, ya, yb, yc, w_br),
        grid=(nd, t // tm),
        in_specs=[pl.BlockSpec((tm, d), lambda n, m: (m, 0)), pl.BlockSpec((tm, LANES), lambda n, m: (m, 0)),
                  *wsp, *bsp, ysp, ysp, ysp,
                  pl.BlockSpec((None, N_BRANCH, BRANCH_WIDTH, tn), lambda n, m: (layer, 0, 0, n))],
        out_specs=[pl.BlockSpec((tm, tn), lambda n, m: (m, n))],
        out_shape=[jax.ShapeDtypeStruct((t, d), BF16)],
        name="gated_merge", casts=casts)
    return out, converted


def _ffn_in_kernel(h_ref, rinv_ref, wg_ref, wu_ref, cw_ref, cb_ref, o_ref, fg_ref, *, tm, tn, rsub, csub,
                   tiles_per_seq):
    seq_start = pl.program_id(1) % tiles_per_seq == 0

    @pl.when(seq_start)
    def _():
        fg_ref[0:FFN_HALO, :] = jnp.zeros((FFN_HALO, tn), F32)

    @pl.when(jnp.logical_not(seq_start))
    def _():
        fg_ref[0:FFN_HALO, :] = fg_ref[tm:tm + FFN_HALO, :]

    def project(r0, c0):
        h = h_ref[r0:r0 + rsub, :]
        rinv = rinv_ref[r0:r0 + rsub, :]
        fg = _row_scale(jnp.dot(h, wg_ref[:, c0:c0 + csub], preferred_element_type=F32), rinv)
        fu = _row_scale(jnp.dot(h, wu_ref[:, c0:c0 + csub], preferred_element_type=F32), rinv)
        fg_ref[FFN_HALO + r0:FFN_HALO + r0 + rsub, c0:c0 + csub] = fg
        return fg, fu

    def finish(r0, c0, fg, fu):
        cols = slice(c0, c0 + csub)
        conv = (cw_ref[0:1, cols] * fg_ref[FFN_HALO - 2 + r0:FFN_HALO - 2 + r0 + rsub, cols]
                + cw_ref[1:2, cols] * fg_ref[FFN_HALO - 1 + r0:FFN_HALO - 1 + r0 + rsub, cols]
                + cw_ref[2:3, cols] * fg + cb_ref[:, cols])
        gelu = 0.5 * conv * (1.0 + lax.erf(conv * (2.0 ** -0.5)))
        o_ref[r0:r0 + rsub, cols] = (gelu * fu).astype(o_ref.dtype)

    pending = None
    for r0 in range(0, tm, rsub):
        for c0 in range(0, tn, csub):
            projected = project(r0, c0)
            if pending is not None:
                finish(*pending)
            pending = (r0, c0, *projected)
    finish(*pending)


def _ffn_in(h, rinv, w_in, layer, conv_w, conv_b, seq, casts=()):
    t, d = h.shape
    f = w_in.shape[2] // 2
    tm, tn = _tile(seq, 1024), _tile(f, 512)
    rsub, csub = _tile(tm, FFN_ROW_SUB), _tile(tn, FFN_COL_SUB)
    nf = f // tn
    kernel = functools.partial(_ffn_in_kernel, tm=tm, tn=tn, rsub=rsub, csub=csub, tiles_per_seq=seq // tm)
    (out,), converted = _pallas_with_casts(
        kernel, (h, rinv, w_in, w_in, conv_w, conv_b.reshape(1, f)),
        grid=(nf, t // tm),
        in_specs=[pl.BlockSpec((tm, d), lambda n, m: (m, 0)),
                  pl.BlockSpec((tm, LANES), lambda n, m: (m, 0)),
                  pl.BlockSpec((None, d, tn), lambda n, m: (layer, 0, n)),
                  pl.BlockSpec((None, d, tn), lambda n, m: (layer, 0, nf + n)),
                  pl.BlockSpec((FFN_CONV_K, tn), lambda n, m: (0, n)),
                  pl.BlockSpec((1, tn), lambda n, m: (0, n))],
        out_specs=[pl.BlockSpec((tm, tn), lambda n, m: (m, n))],
        out_shape=[jax.ShapeDtypeStruct((t, f), BF16)],
        scratch_shapes=[pltpu.VMEM((FFN_HALO + tm, tn), F32)],
        name="ffn_in", casts=casts)
    return out, converted


def kernel(x, norm_mix, w_in, w_gate, b_gate, diff_q_norm, diff_k_norm, diff_lambda, diff_out_norm,
           ret_out_norm, conv_w, conv_b, conv_ln_g, conv_ln_b, w_br, w_o, norm_ffn, w_ffn_in,
           ffn_conv_w, ffn_conv_b, w_ffn_out):
    bsz, seq, d = x.shape
    depth = w_in.shape[0]
    t = bsz * seq
    x = x.reshape(t, d)
    wi, wg, wb, wo = (w[0:1].astype(BF16) for w in (w_in, w_gate, w_br, w_o))
    h, rinv = _norm_prep(x, norm_mix[0])
    for l in range(depth):
        lam_init = 0.8 - 0.6 * math.exp(-0.3 * l)
        nxt = l + 1 < depth
        z, (wfi,) = _matmul(h, rinv, wi, 0, BF16, casts=((w_ffn_in, l),))
        ya = _retention(z, ret_out_norm[l], bsz, seq)
        qt, kd, vt = _qk_norm(z, diff_q_norm[l], diff_k_norm[l])
        yb = _diff_attention(qt, kd, vt, diff_lambda[l], diff_out_norm[l], lam_init, bsz, seq)
        yc = _conv_module(z, conv_w[l], conv_b[l], conv_ln_g[l], conv_ln_b[l], bsz, seq)
        merged, (wfo,) = _gated_merge(h, rinv, wg, b_gate[l].reshape(1, N_BRANCH * d), ya, yb, yc, wb, 0,
                                      casts=((w_ffn_out, l),))
        (x, h, rinv), _ = _matmul_residual(merged, wo, 0, x, norm_ffn[l])
        act, nxt_in = _ffn_in(h, rinv, wfi, 0, ffn_conv_w[l], ffn_conv_b[l], seq,
                              casts=((w_in, l + 1), (w_o, l + 1)) if nxt else ())
        if nxt:
            (x, h, rinv), (wg, wb) = _matmul_residual(act, wfo, 0, x, norm_mix[l + 1],
                                                      casts=((w_gate, l + 1), (w_br, l + 1)))
            wi, wo = nxt_in
        else:
            x = _matmul_residual(act, wfo, 0, x)
    return x.reshape(bsz, seq, d)
```

```python
import functools
import math

import jax
import jax.numpy as jnp
from jax import lax
from jax.experimental import pallas as pl
from jax.experimental.pallas import tpu as pltpu

EPS = 1e-6
NEG_INF = -1e30
CHUNK = 64
LOG2E = 1.4426950408889634

RET_HEADS = 4
RET_DK = 128
RET_DV = 256
DIFF_HEADS = 4
DIFF_DH = 128
CONV_CH = 1024
CONV_K = 31
FFN_CONV_K = 3
BRANCH_WIDTH = 1024
N_BRANCH = 3

COL_RQ = 0
COL_RK = COL_RQ + RET_HEADS * RET_DK
COL_RV = COL_RK + RET_HEADS * RET_DK
COL_RG = COL_RV + RET_HEADS * RET_DV
COL_DQ = COL_RG + RET_HEADS * RET_DV
COL_DK = COL_DQ + DIFF_HEADS * 2 * DIFF_DH
COL_DV = COL_DK + DIFF_HEADS * 2 * DIFF_DH
COL_CA = COL_DV + DIFF_HEADS * 2 * DIFF_DH
COL_CB = COL_CA + CONV_CH

LANES = 128
CAST_ROWS = 16
VMEM_LIMIT_BYTES = 56 * 1024 * 1024
CONV_HALO = 32
FFN_HALO = 8
FFN_ROW_SUB = 512
FFN_COL_SUB = 512

F32 = jnp.float32
BF16 = jnp.bfloat16


def _params(*sem):
    return pltpu.CompilerParams(dimension_semantics=sem, vmem_limit_bytes=VMEM_LIMIT_BYTES)


def _tile(n, pref):
    t = min(n, pref)
    assert n % t == 0, (n, pref)
    return t


def _pallas_with_casts(kernel, args, *, grid, in_specs, out_specs, out_shape, name, scratch_shapes=(), casts=()):
    out_specs, out_shape = list(out_specs), list(out_shape)
    n_in, n_out, n_cast = len(in_specs), len(out_shape), len(casts)
    inner = grid[1]
    steps = grid[0] * grid[1]
    views, slab_in, slab_out = [], [], []
    for w, layer in casts:
        v = w.reshape(w.shape[0], -1, w.shape[-1])
        rows, cols = v.shape[1:]
        slab = CAST_ROWS * pl.cdiv(rows, CAST_ROWS * steps)
        assert rows % slab == 0, (w.shape, grid)
        last = rows // slab - 1
        views.append(v)
        slab_in.append(pl.BlockSpec((None, slab, cols), lambda i, j, layer=layer, last=last:
                                    (layer, jnp.minimum(i * inner + j, last), 0)))
        slab_out.append(pl.BlockSpec((slab, cols), lambda i, j, last=last: (jnp.minimum(i * inner + j, last), 0)))

    def body(*refs):
        ins, refs = refs[:n_in], refs[n_in:]
        cast_in, refs = refs[:n_cast], refs[n_cast:]
        outs, refs = refs[:n_out], refs[n_out:]
        cast_out, scratch = refs[:n_cast], refs[n_cast:]
        kernel(*ins, *outs, *scratch)
        for src, dst in zip(cast_in, cast_out):
            dst[...] = src[...].astype(dst.dtype)

    res = pl.pallas_call(
        body,
        grid=grid,
        in_specs=[*in_specs, *slab_in],
        out_specs=[*out_specs, *slab_out],
        out_shape=[*out_shape, *[jax.ShapeDtypeStruct(v.shape[1:], BF16) for v in views]],
        scratch_shapes=list(scratch_shapes),
        compiler_params=_params("arbitrary", "arbitrary"),
        name=name,
    )(*args, *views)
    converted = [c.reshape((1,) + w.shape[1:]) for c, (w, _) in zip(res[n_out:], casts)]
    return res[:n_out], converted


def _lane_fold(v):
    out = v[:, 0:LANES]
    for c in range(1, v.shape[1] // LANES):
        out = out + v[:, c * LANES:(c + 1) * LANES]
    return out


def _rinv_from_folded(ss, d):
    total = jnp.sum(ss, axis=-1, keepdims=True)
    return jnp.broadcast_to(lax.rsqrt(total * (1.0 / d) + EPS), ss.shape)


def _row_scale(acc, rinv):
    return acc * jnp.tile(rinv, (1, acc.shape[1] // LANES))


def _norm_prep_kernel(x_ref, g_ref, xg_ref, rinv_ref):
    x = x_ref[...]
    xg_ref[...] = (x * g_ref[...]).astype(xg_ref.dtype)
    rinv_ref[...] = _rinv_from_folded(_lane_fold(x * x), x.shape[1])


def _norm_prep(x, g):
    t, d = x.shape
    tr = _tile(t, 256)
    return pl.pallas_call(
        _norm_prep_kernel,
        grid=(t // tr,),
        in_specs=[pl.BlockSpec((tr, d), lambda i: (i, 0)),
                  pl.BlockSpec((1, d), lambda i: (0, 0))],
        out_specs=[pl.BlockSpec((tr, d), lambda i: (i, 0)), pl.BlockSpec((tr, LANES), lambda i: (i, 0))],
        out_shape=[jax.ShapeDtypeStruct((t, d), BF16), jax.ShapeDtypeStruct((t, LANES), F32)],
        compiler_params=_params("arbitrary"),
        name="norm_prep",
    )(x, g.reshape(1, d))


def _mm_kernel(a_ref, rinv_ref, b_ref, o_ref):
    acc = jnp.dot(a_ref[...], b_ref[...], preferred_element_type=F32)
    o_ref[...] = _row_scale(acc, rinv_ref[...]).astype(o_ref.dtype)


def _matmul(a, rinv, b, layer, out_dtype, casts=()):
    m, k = a.shape
    n = b.shape[2]
    tm, tn = _tile(m, 1024), _tile(n, 1024)
    (out,), converted = _pallas_with_casts(
        _mm_kernel, (a, rinv, b),
        grid=(m // tm, n // tn),
        in_specs=[pl.BlockSpec((tm, k), lambda i, j: (i, 0)),
                  pl.BlockSpec((tm, LANES), lambda i, j: (i, 0)),
                  pl.BlockSpec((None, k, tn), lambda i, j: (layer, 0, j))],
        out_specs=[pl.BlockSpec((tm, tn), lambda i, j: (i, j))],
        out_shape=[jax.ShapeDtypeStruct((m, n), out_dtype)],
        name="matmul", casts=casts)
    return out, converted


def _mm_res_kernel(a_ref, b_ref, r_ref, o_ref):
    o_ref[...] = r_ref[...] + jnp.dot(a_ref[...], b_ref[...], preferred_element_type=F32)


def _mm_res_norm_kernel(a_ref, b_ref, r_ref, g_ref, o_ref, xg_ref, rinv_ref, ss_ref, *, d, rsub):
    @pl.when(pl.program_id(1) == 0)
    def _():
        ss_ref[...] = jnp.zeros_like(ss_ref)

    def project(r0):
        return r_ref[r0:r0 + rsub, :] + jnp.dot(a_ref[r0:r0 + rsub, :], b_ref[...], preferred_element_type=F32)

    def finish(r0, x):
        rows = slice(r0, r0 + rsub)
        o_ref[rows, :] = x
        xg_ref[rows, :] = (x * g_ref[...]).astype(xg_ref.dtype)
        ss = _lane_fold(x * x) + ss_ref[rows, :]
        ss_ref[rows, :] = ss
        rinv_ref[rows, :] = _rinv_from_folded(ss, d)

    pending = None
    for r0 in range(0, a_ref.shape[0], rsub):
        x = project(r0)
        if pending is not None:
            finish(*pending)
        pending = (r0, x)
    finish(*pending)


def _matmul_residual(a, b, layer, res, g_next=None, casts=()):
    m, k = a.shape
    n = b.shape[2]
    tm = _tile(m, 1024)
    tn = _tile(n, (2 * 1024 * 1024) // k)
    in_specs = [pl.BlockSpec((tm, k), lambda i, j: (i, 0)),
                pl.BlockSpec((None, k, tn), lambda i, j: (layer, 0, j)),
                pl.BlockSpec((tm, tn), lambda i, j: (i, j))]
    tile = pl.BlockSpec((tm, tn), lambda i, j: (i, j))
    if g_next is None:
        return pl.pallas_call(
            _mm_res_kernel,
            grid=(m // tm, n // tn),
            in_specs=in_specs,
            out_specs=tile,
            out_shape=jax.ShapeDtypeStruct((m, n), F32),
            compiler_params=_params("arbitrary", "arbitrary"),
            name="matmul_residual",
        )(a, b, res)
    return _pallas_with_casts(
        functools.partial(_mm_res_norm_kernel, d=n, rsub=_tile(tm, 512)), (a, b, res, g_next.reshape(1, n)),
        grid=(m // tm, n // tn),
        in_specs=in_specs + [pl.BlockSpec((1, tn), lambda i, j: (0, j))],
        out_specs=[tile, tile, pl.BlockSpec((tm, LANES), lambda i, j: (i, 0))],
        out_shape=[jax.ShapeDtypeStruct((m, n), F32), jax.ShapeDtypeStruct((m, n), BF16),
                   jax.ShapeDtypeStruct((m, LANES), F32)],
        scratch_shapes=[pltpu.VMEM((tm, LANES), F32)],
        name="matmul_residual_norm", casts=casts)


def _retention_kernel(q_ref, k_ref, v_ref, g_ref, mask_ref, qdec_ref, kdec_ref, cdec_ref, gn_ref,
                      o_ref, state_ref):
    @pl.when(pl.program_id(1) == 0)
    def _():
        state_ref[...] = jnp.zeros_like(state_ref)

    heads = range(RET_HEADS)
    q = [q_ref[:, h * RET_DK:(h + 1) * RET_DK] for h in heads]
    k = [k_ref[:, h * RET_DK:(h + 1) * RET_DK] for h in heads]
    v = [v_ref[:, h * RET_DV:(h + 1) * RET_DV] for h in heads]
    scores = [lax.dot_general(q[h], k[h], (((1,), (1,)), ((), ())), preferred_element_type=F32) * mask_ref[h]
              for h in heads]
    state = [state_ref[h] for h in heads]
    cross = [jnp.dot((q[h].astype(F32) * qdec_ref[h]).astype(BF16), state[h].astype(BF16),
                     preferred_element_type=F32) for h in heads]
    inner = [jnp.dot(scores[h].astype(BF16), v[h], preferred_element_type=F32) for h in heads]
    for h in heads:
        kd = (k[h].astype(F32) * kdec_ref[h]).astype(BF16)
        kv = lax.dot_general(kd, v[h], (((0,), (0,)), ((), ())), preferred_element_type=F32)
        state_ref[h] = state[h] * cdec_ref[h] + kv
    for h in heads:
        out = inner[h] + cross[h]
        ms = jnp.mean(out * out, axis=-1, keepdims=True)
        y = out * lax.rsqrt(ms + EPS) * gn_ref[h]
        g = g_ref[:, h * RET_DV:(h + 1) * RET_DV].astype(F32)
        o_ref[:, h * RET_DV:(h + 1) * RET_DV] = (g * jax.nn.sigmoid(g) * y).astype(o_ref.dtype)


def _retention_tables(blk):
    lg = jnp.log(1.0 - 2.0 ** (-5.0 - jnp.arange(RET_HEADS, dtype=F32)))
    pos = jnp.arange(blk, dtype=F32)
    dist = jnp.abs(pos[:, None] - pos[None, :])
    allowed = (jnp.arange(blk)[None, :] // CHUNK) <= (jnp.arange(blk)[:, None] // CHUNK)
    kscale = RET_DK ** -0.5
    mask = jnp.where(allowed[None], jnp.exp(dist[None] * lg[:, None, None]), 0.0) * kscale
    qdec = jnp.exp((pos[None, :] + 1.0) * lg[:, None])
    kdec = jnp.exp((blk - 1.0 - pos[None, :]) * lg[:, None]) * kscale
    cdec = jnp.exp(blk * lg)
    qdec = jnp.broadcast_to(qdec[:, :, None], (RET_HEADS, blk, RET_DK))
    kdec = jnp.broadcast_to(kdec[:, :, None], (RET_HEADS, blk, RET_DK))
    cdec = jnp.broadcast_to(cdec[:, None, None], (RET_HEADS, 1, RET_DV))
    return mask, qdec, kdec, cdec


def _retention(z, out_norm, bsz, seq):
    t = z.shape[0]
    blk = _tile(seq, 256)
    nblk = seq // blk
    mask, qdec, kdec, cdec = _retention_tables(blk)
    qk_w, v_w = RET_HEADS * RET_DK, RET_HEADS * RET_DV
    whole = lambda shape: pl.BlockSpec(shape, lambda b, i: (0,) * len(shape))
    return pl.pallas_call(
        _retention_kernel,
        grid=(bsz, nblk),
        in_specs=[
            pl.BlockSpec((blk, qk_w), lambda b, i: (b * nblk + i, COL_RQ // qk_w)),
            pl.BlockSpec((blk, qk_w), lambda b, i: (b * nblk + i, COL_RK // qk_w)),
            pl.BlockSpec((blk, v_w), lambda b, i: (b * nblk + i, COL_RV // v_w)),
            pl.BlockSpec((blk, v_w), lambda b, i: (b * nblk + i, COL_RG // v_w)),
            whole((RET_HEADS, blk, blk)),
            whole((RET_HEADS, blk, RET_DK)),
            whole((RET_HEADS, blk, RET_DK)),
            whole((RET_HEADS, 1, RET_DV)),
            whole((RET_HEADS, 1, RET_DV)),
        ],
        out_specs=pl.BlockSpec((blk, v_w), lambda b, i: (b * nblk + i, 0)),
        out_shape=jax.ShapeDtypeStruct((t, v_w), BF16),
        scratch_shapes=[pltpu.VMEM((RET_HEADS, RET_DK, RET_DV), F32)],
        compiler_params=_params("arbitrary", "arbitrary"),
        name="retention",
    )(z, z, z, z, mask, qdec, kdec, cdec, out_norm.reshape(RET_HEADS, 1, RET_DV))


def _qk_norm_kernel(q_ref, k_ref, gq_ref, gk_ref, qt_ref, ko_ref):
    gq = gq_ref[...] * (DIFF_DH ** -0.5 * LOG2E)
    gk = gk_ref[...]
    for c in range(q_ref.shape[1] // DIFF_DH):
        cols = slice(c * DIFF_DH, (c + 1) * DIFF_DH)
        q = q_ref[:, cols].astype(F32)
        q = q * lax.rsqrt(jnp.mean(q * q, axis=-1, keepdims=True) + EPS) * gq
        qt_ref[cols, :] = q.T.astype(qt_ref.dtype)
        k = k_ref[:, cols].astype(F32)
        k = k * lax.rsqrt(jnp.mean(k * k, axis=-1, keepdims=True) + EPS) * gk
        ko_ref[:, cols] = k.astype(ko_ref.dtype)


def _qk_norm(z, gq, gk):
    t = z.shape[0]
    w = DIFF_HEADS * 2 * DIFF_DH
    tr = _tile(t, 512)
    spec = lambda col: pl.BlockSpec((tr, w), lambda i: (i, col // w))
    vec = pl.BlockSpec((1, DIFF_DH), lambda i: (0, 0))
    rows = pl.BlockSpec((tr, w), lambda i: (i, 0))
    cols = pl.BlockSpec((w, tr), lambda i: (0, i))
    return pl.pallas_call(
        _qk_norm_kernel,
        grid=(t // tr,),
        in_specs=[spec(COL_DQ), spec(COL_DK), vec, vec],
        out_specs=[cols, rows],
        out_shape=[jax.ShapeDtypeStruct((w, t), BF16), jax.ShapeDtypeStruct((t, w), BF16)],
        compiler_params=_params("arbitrary"),
        name="qk_norm",
    )(z, z, gq.reshape(1, DIFF_DH), gk.reshape(1, DIFF_DH))


def _diff_attn_kernel(qt_ref, k_ref, v_ref, nb_ref, slope_ref, lam_ref, gn_ref, o_ref,
                      m_sc, l_sc, acc_sc, p_sc, a_sc, *, blk, lam_init):
    i = pl.program_id(2)
    m_sc[...] = jnp.full_like(m_sc, NEG_INF)
    l_sc[...] = jnp.zeros_like(l_sc)
    acc_sc[...] = jnp.zeros_like(acc_sc)

    def scores(j):
        ks = k_ref[pl.ds(pl.multiple_of(j * blk, blk), blk), :]
        bias = nb_ref[0, (j == i).astype(jnp.int32)]
        return [jnp.dot(ks[:, m * DIFF_DH:(m + 1) * DIFF_DH], qt_ref[m * DIFF_DH:(m + 1) * DIFF_DH, :],
                        preferred_element_type=F32) + bias for m in range(2)]

    def softmax_update(j, slot, s1s):
        soff = slope_ref[0] * ((i - j) * blk).astype(F32)
        for m in range(2):
            m_old = m_sc[m]
            m_new = jnp.maximum(m_old, jnp.max(s1s[m], axis=0, keepdims=True) - soff)
            alpha = jnp.exp2(m_old - m_new)
            p = jnp.exp2(s1s[m] - (m_new + soff))
            l_sc[m] = alpha * l_sc[m] + jnp.sum(p, axis=0, keepdims=True)
            m_sc[m] = m_new
            p_sc[slot, m] = p.astype(BF16)
            a_sc[slot, m] = jnp.broadcast_to(alpha, (8, blk)).T

    def add_values(j, slot):
        vs = v_ref[pl.ds(pl.multiple_of(j * blk, blk), blk), :]
        for m in range(2):
            pv = lax.dot_general(p_sc[slot, m], vs, (((0,), (0,)), ((), ())), preferred_element_type=F32)
            acc_sc[m] = a_sc[slot, m][:, 0:1] * acc_sc[m] + pv

    def step(j, slot):
        s1s = scores(j)
        add_values(j - 1, 1 - slot)
        softmax_update(j, slot, s1s)

    softmax_update(0, 0, scores(0))

    def pair(t, carry):
        step(2 * t + 1, 1)
        step(2 * t + 2, 0)
        return carry

    lax.fori_loop(0, i // 2, pair, 0)

    @pl.when(i % 2 == 1)
    def _():
        step(i, 1)
        add_values(i, 1)

    @pl.when(i % 2 == 0)
    def _():
        add_values(i, 0)

    lv = lam_ref[...]
    lam = (jnp.exp(jnp.sum(lv[0:1] * lv[1:2], axis=-1, keepdims=True))
           - jnp.exp(jnp.sum(lv[2:3] * lv[3:4], axis=-1, keepdims=True)) + lam_init)
    l_col = [jnp.broadcast_to(l_sc[m], (8, blk)).T[:, 0:1] for m in range(2)]
    out = acc_sc[0] / l_col[0] - lam * (acc_sc[1] / l_col[1])
    ms = jnp.mean(out * out, axis=-1, keepdims=True)
    o_ref[...] = (out * lax.rsqrt(ms + EPS) * gn_ref[0] * (1.0 - lam_init)).astype(o_ref.dtype)


def _diff_attention(qt, kd, z, lam_params, out_norm, lam_init, bsz, seq):
    t = kd.shape[0]
    hw = 2 * DIFF_DH
    blk = _tile(seq, 512)
    nblk = seq // blk
    slopes = 2.0 ** (-8.0 * (jnp.arange(DIFF_HEADS, dtype=F32) + 1.0) / DIFF_HEADS) * LOG2E
    r = jnp.arange(blk)[None, :]
    c = jnp.arange(blk)[:, None]
    past = -slopes[:, None, None] * (r - c).astype(F32)[None]
    allowed = (c // CHUNK) <= (r // CHUNK)
    diag = jnp.where(allowed[None], -slopes[:, None, None] * jnp.abs(r - c).astype(F32)[None], NEG_INF)
    nb = jnp.stack([past, diag], axis=1)
    slope_rows = jnp.broadcast_to(slopes[:, None, None], (DIFF_HEADS, 1, blk))
    kernel = functools.partial(_diff_attn_kernel, blk=blk, lam_init=lam_init)
    return pl.pallas_call(
        kernel,
        grid=(bsz, DIFF_HEADS, nblk),
        in_specs=[
            pl.BlockSpec((hw, blk), lambda b, h, i: (h, b * nblk + i)),
            pl.BlockSpec((seq, hw), lambda b, h, i: (b, h)),
            pl.BlockSpec((seq, hw), lambda b, h, i: (b, COL_DV // hw + h)),
            pl.BlockSpec((1, 2, blk, blk), lambda b, h, i: (h, 0, 0, 0)),
            pl.BlockSpec((1, 1, blk), lambda b, h, i: (h, 0, 0)),
            pl.BlockSpec((4, DIFF_DH), lambda b, h, i: (0, 0)),
            pl.BlockSpec((1, 1, hw), lambda b, h, i: (h, 0, 0)),
        ],
        out_specs=pl.BlockSpec((blk, hw), lambda b, h, i: (b * nblk + i, h)),
        out_shape=jax.ShapeDtypeStruct((t, DIFF_HEADS * hw), BF16),
        scratch_shapes=[pltpu.VMEM((2, 1, blk), F32), pltpu.VMEM((2, 1, blk), F32),
                        pltpu.VMEM((2, blk, hw), F32), pltpu.VMEM((2, 2, blk, blk), BF16),
                        pltpu.VMEM((2, 2, blk, 8), F32)],
        compiler_params=_params("arbitrary", "arbitrary", "arbitrary"),
        name="diff_attention",
    )(qt, kd, z, nb, slope_rows, lam_params, out_norm.reshape(DIFF_HEADS, 1, hw))


def _conv_module_kernel(ca_ref, cb_ref, cah_ref, cbh_ref, w_ref, b_ref, g_ref, beta_ref, o_ref, xe_ref,
                        xs_ref, *, rows, sub):
    first = pl.program_id(1) == 0
    halo = cah_ref[...].astype(F32) * jax.nn.sigmoid(cbh_ref[...].astype(F32))
    xe_ref[0:CONV_HALO, :] = jnp.where(first, 0.0, halo)
    xe_ref[CONV_HALO:CONV_HALO + rows, :] = ca_ref[...].astype(F32) * jax.nn.sigmoid(cb_ref[...].astype(F32))

    base = CONV_HALO - (CONV_K - 1)
    span = xs_ref.shape[1]
    for s in range(1, 8):
        xs_ref[s - 1] = xe_ref[s:s + span, :]

    for r0 in range(0, rows, sub):
        acc = jnp.broadcast_to(b_ref[...], (sub, CONV_CH))
        for k in range(CONV_K):
            a, s = divmod(base + k, 8)
            src = xe_ref if s == 0 else xs_ref.at[s - 1]
            acc = acc + jnp.tile(w_ref[k], (sub // 8, 1)) * src[r0 + 8 * a:r0 + 8 * a + sub, :]
        mu = jnp.mean(acc, axis=-1, keepdims=True)
        xc = acc - mu
        var = jnp.mean(xc * xc, axis=-1, keepdims=True)
        y = xc * lax.rsqrt(var + EPS) * g_ref[...] + beta_ref[...]
        o_ref[r0:r0 + sub, :] = (y * jax.nn.sigmoid(y)).astype(o_ref.dtype)


def _conv_module(z, w, b, ln_g, ln_b, bsz, seq):
    t = z.shape[0]
    rows = _tile(seq, 256)
    nblk = seq // rows
    hpb = rows // CONV_HALO
    main = lambda col: pl.BlockSpec((rows, CONV_CH), lambda bb, i: (bb * nblk + i, col // CONV_CH))
    halo = lambda col: pl.BlockSpec(
        (CONV_HALO, CONV_CH), lambda bb, i: (jnp.maximum((bb * nblk + i) * hpb - 1, 0), col // CONV_CH))
    vec = pl.BlockSpec((1, CONV_CH), lambda bb, i: (0, 0))
    kernel = functools.partial(_conv_module_kernel, rows=rows, sub=32)
    return pl.pallas_call(
        kernel,
        grid=(bsz, nblk),
        in_specs=[main(COL_CA), main(COL_CB), halo(COL_CA), halo(COL_CB),
                  pl.BlockSpec((CONV_K, 8, CONV_CH), lambda bb, i: (0, 0, 0)), vec, vec, vec],
        out_specs=pl.BlockSpec((rows, CONV_CH), lambda bb, i: (bb * nblk + i, 0)),
        out_shape=jax.ShapeDtypeStruct((t, CONV_CH), BF16),
        scratch_shapes=[pltpu.VMEM((CONV_HALO + rows, CONV_CH), F32),
                        pltpu.VMEM((7, CONV_HALO + rows - 8, CONV_CH), F32)],
        compiler_params=_params("arbitrary", "arbitrary"),
        name="conv_module",
    )(z, z, z, z, jnp.broadcast_to(w[:, None, :], (CONV_K, 8, CONV_CH)), b.reshape(1, CONV_CH),
      ln_g.reshape(1, CONV_CH), ln_b.reshape(1, CONV_CH))


def _merge_kernel(h_ref, rinv_ref, wg0_ref, wg1_ref, wg2_ref, bg0_ref, bg1_ref, bg2_ref, ya_ref, yb_ref,
                  yc_ref, wbr_ref, o_ref):
    h = h_ref[...]
    rinv = jnp.tile(rinv_ref[...], (1, o_ref.shape[1] // LANES))
    merged = None
    for j, (wg_ref, bg_ref, y_ref) in enumerate(((wg0_ref, bg0_ref, ya_ref), (wg1_ref, bg1_ref, yb_ref),
                                                 (wg2_ref, bg2_ref, yc_ref))):
        gate = jax.nn.sigmoid(jnp.dot(h, wg_ref[...], preferred_element_type=F32) * rinv + bg_ref[...])
        term = gate * jnp.dot(y_ref[...], wbr_ref[j], preferred_element_type=F32)
        merged = term if merged is None else merged + term
    o_ref[...] = merged.astype(o_ref.dtype)


def _gated_merge(h, rinv, w_gate, b_gate, ya, yb, yc, w_br, layer, casts=()):
    t, d = h.shape
    tm, tn = _tile(t, 512), _tile(d, 512)
    nd = d // tn
    ysp = pl.BlockSpec((tm, BRANCH_WIDTH), lambda n, m: (m, 0))
    wsp = [pl.BlockSpec((None, d, tn), lambda n, m, j=j: (layer, 0, j * nd + n)) for j in range(N_BRANCH)]
    bsp = [pl.BlockSpec((1, tn), lambda n, m, j=j: (0, j * nd + n)) for j in range(N_BRANCH)]
    (out,), converted = _pallas_with_casts(
        _merge_kernel, (h, rinv, w_gate, w_gate, w_gate, b_gate, b_gate, b_gate, ya, yb, yc, w_br),
        grid=(nd, t // tm),
        in_specs=[pl.BlockSpec((tm, d), lambda n, m: (m, 0)), pl.BlockSpec((tm, LANES), lambda n, m: (m, 0)),
                  *wsp, *bsp, ysp, ysp, ysp,
                  pl.BlockSpec((None, N_BRANCH, BRANCH_WIDTH, tn), lambda n, m: (layer, 0, 0, n))],
        out_specs=[pl.BlockSpec((tm, tn), lambda n, m: (m, n))],
        out_shape=[jax.ShapeDtypeStruct((t, d), BF16)],
        name="gated_merge", casts=casts)
    return out, converted


def _ffn_in_kernel(h_ref, rinv_ref, wg_ref, wu_ref, cw_ref, cb_ref, o_ref, fg_ref, *, tm, tn, rsub, csub,
                   tiles_per_seq):
    seq_start = pl.program_id(1) % tiles_per_seq == 0

    @pl.when(seq_start)
    def _():
        fg_ref[0:FFN_HALO, :] = jnp.zeros((FFN_HALO, tn), F32)

    @pl.when(jnp.logical_not(seq_start))
    def _():
        fg_ref[0:FFN_HALO, :] = fg_ref[tm:tm + FFN_HALO, :]

    def project(r0, c0):
        h = h_ref[r0:r0 + rsub, :]
        rinv = rinv_ref[r0:r0 + rsub, :]
        fg = _row_scale(jnp.dot(h, wg_ref[:, c0:c0 + csub], preferred_element_type=F32), rinv)
        fu = _row_scale(jnp.dot(h, wu_ref[:, c0:c0 + csub], preferred_element_type=F32), rinv)
        fg_ref[FFN_HALO + r0:FFN_HALO + r0 + rsub, c0:c0 + csub] = fg
        return fg, fu

    def finish(r0, c0, fg, fu):
        cols = slice(c0, c0 + csub)
        conv = (cw_ref[0:1, cols] * fg_ref[FFN_HALO - 2 + r0:FFN_HALO - 2 + r0 + rsub, cols]
                + cw_ref[1:2, cols] * fg_ref[FFN_HALO - 1 + r0:FFN_HALO - 1 + r0 + rsub, cols]
                + cw_ref[2:3, cols] * fg + cb_ref[:, cols])
        gelu = 0.5 * conv * (1.0 + lax.erf(conv * (2.0 ** -0.5)))
        o_ref[r0:r0 + rsub, cols] = (gelu * fu).astype(o_ref.dtype)

    pending = None
    for r0 in range(0, tm, rsub):
        for c0 in range(0, tn, csub):
            projected = project(r0, c0)
            if pending is not None:
                finish(*pending)
            pending = (r0, c0, *projected)
    finish(*pending)


def _ffn_in(h, rinv, w_in, layer, conv_w, conv_b, seq, casts=()):
    t, d = h.shape
    f = w_in.shape[2] // 2
    tm, tn = _tile(seq, 1024), _tile(f, 512)
    rsub, csub = _tile(tm, FFN_ROW_SUB), _tile(tn, FFN_COL_SUB)
    nf = f // tn
    kernel = functools.partial(_ffn_in_kernel, tm=tm, tn=tn, rsub=rsub, csub=csub, tiles_per_seq=seq // tm)
    (out,), converted = _pallas_with_casts(
        kernel, (h, rinv, w_in, w_in, conv_w, conv_b.reshape(1, f)),
        grid=(nf, t // tm),
        in_specs=[pl.BlockSpec((tm, d), lambda n, m: (m, 0)),
                  pl.BlockSpec((tm, LANES), lambda n, m: (m, 0)),
                  pl.BlockSpec((None, d, tn), lambda n, m: (layer, 0, n)),
                  pl.BlockSpec((None, d, tn), lambda n, m: (layer, 0, nf + n)),
                  pl.BlockSpec((FFN_CONV_K, tn), lambda n, m: (0, n)),
                  pl.BlockSpec((1, tn), lambda n, m: (0, n))],
        out_specs=[pl.BlockSpec((tm, tn), lambda n, m: (m, n))],
        out_shape=[jax.ShapeDtypeStruct((t, f), BF16)],
        scratch_shapes=[pltpu.VMEM((FFN_HALO + tm, tn), F32)],
        name="ffn_in", casts=casts)
    return out, converted


def kernel(x, norm_mix, w_in, w_gate, b_gate, diff_q_norm, diff_k_norm, diff_lambda, diff_out_norm,
           ret_out_norm, conv_w, conv_b, conv_ln_g, conv_ln_b, w_br, w_o, norm_ffn, w_ffn_in,
           ffn_conv_w, ffn_conv_b, w_ffn_out):
    bsz, seq, d = x.shape
    depth = w_in.shape[0]
    t = bsz * seq
    x = x.reshape(t, d)
    wi, wg, wb, wo = (w[0:1].astype(BF16) for w in (w_in, w_gate, w_br, w_o))
    h, rinv = _norm_prep(x, norm_mix[0])
    for l in range(depth):
        lam_init = 0.8 - 0.6 * math.exp(-0.3 * l)
        nxt = l + 1 < depth
        z, (wfi,) = _matmul(h, rinv, wi, 0, BF16, casts=((w_ffn_in, l),))
        ya = _retention(z, ret_out_norm[l], bsz, seq)
        qt, kd = _qk_norm(z, diff_q_norm[l], diff_k_norm[l])
        yb = _diff_attention(qt, kd, z, diff_lambda[l], diff_out_norm[l], lam_init, bsz, seq)
        yc = _conv_module(z, conv_w[l], conv_b[l], conv_ln_g[l], conv_ln_b[l], bsz, seq)
        merged, (wfo,) = _gated_merge(h, rinv, wg, b_gate[l].reshape(1, N_BRANCH * d), ya, yb, yc, wb, 0,
                                      casts=((w_ffn_out, l),))
        (x, h, rinv), _ = _matmul_residual(merged, wo, 0, x, norm_ffn[l])
        act, nxt_in = _ffn_in(h, rinv, wfi, 0, ffn_conv_w[l], ffn_conv_b[l], seq,
                              casts=((w_in, l + 1), (w_o, l + 1)) if nxt else ())
        if nxt:
            (x, h, rinv), (wg, wb) = _matmul_residual(act, wfo, 0, x, norm_mix[l + 1],
                                                      casts=((w_gate, l + 1), (w_br, l + 1)))
            wi, wo = nxt_in
        else:
            x = _matmul_residual(act, wfo, 0, x)
    return x.reshape(bsz, seq, d)
```

```python
import functools
import math

import jax
import jax.numpy as jnp
from jax import lax
from jax.experimental import pallas as pl
from jax.experimental.pallas import tpu as pltpu

EPS = 1e-6
NEG_INF = -1e30
CHUNK = 64
LOG2E = 1.4426950408889634

RET_HEADS = 4
RET_DK = 128
RET_DV = 256
DIFF_HEADS = 4
DIFF_DH = 128
CONV_CH = 1024
CONV_K = 31
FFN_CONV_K = 3
BRANCH_WIDTH = 1024
N_BRANCH = 3

COL_RQ = 0
COL_RK = COL_RQ + RET_HEADS * RET_DK
COL_RV = COL_RK + RET_HEADS * RET_DK
COL_RG = COL_RV + RET_HEADS * RET_DV
COL_DQ = COL_RG + RET_HEADS * RET_DV
COL_DK = COL_DQ + DIFF_HEADS * 2 * DIFF_DH
COL_DV = COL_DK + DIFF_HEADS * 2 * DIFF_DH
COL_CA = COL_DV + DIFF_HEADS * 2 * DIFF_DH
COL_CB = COL_CA + CONV_CH

LANES = 128
CAST_ROWS = 16
VMEM_LIMIT_BYTES = 56 * 1024 * 1024
CONV_HALO = 32
FFN_HALO = 8
FFN_ROW_SUB = 512
FFN_COL_SUB = 512

F32 = jnp.float32
BF16 = jnp.bfloat16


def _params(*sem):
    return pltpu.CompilerParams(dimension_semantics=sem, vmem_limit_bytes=VMEM_LIMIT_BYTES)


def _tile(n, pref):
    t = min(n, pref)
    assert n % t == 0, (n, pref)
    return t


def _pallas_with_casts(kernel, args, *, grid, in_specs, out_specs, out_shape, name, scratch_shapes=(), casts=()):
    out_specs, out_shape = list(out_specs), list(out_shape)
    n_in, n_out, n_cast = len(in_specs), len(out_shape), len(casts)
    inner = grid[1]
    steps = grid[0] * grid[1]
    views, slab_in, slab_out = [], [], []
    for w, layer in casts:
        v = w.reshape(w.shape[0], -1, w.shape[-1])
        rows, cols = v.shape[1:]
        slab = CAST_ROWS * pl.cdiv(rows, CAST_ROWS * steps)
        assert rows % slab == 0, (w.shape, grid)
        last = rows // slab - 1
        views.append(v)
        slab_in.append(pl.BlockSpec((None, slab, cols), lambda i, j, layer=layer, last=last:
                                    (layer, jnp.minimum(i * inner + j, last), 0)))
        slab_out.append(pl.BlockSpec((slab, cols), lambda i, j, last=last: (jnp.minimum(i * inner + j, last), 0)))

    def body(*refs):
        ins, refs = refs[:n_in], refs[n_in:]
        cast_in, refs = refs[:n_cast], refs[n_cast:]
        outs, refs = refs[:n_out], refs[n_out:]
        cast_out, scratch = refs[:n_cast], refs[n_cast:]
        kernel(*ins, *outs, *scratch)
        for src, dst in zip(cast_in, cast_out):
            dst[...] = src[...].astype(dst.dtype)

    res = pl.pallas_call(
        body,
        grid=grid,
        in_specs=[*in_specs, *slab_in],
        out_specs=[*out_specs, *slab_out],
        out_shape=[*out_shape, *[jax.ShapeDtypeStruct(v.shape[1:], BF16) for v in views]],
        scratch_shapes=list(scratch_shapes),
        compiler_params=_params("arbitrary", "arbitrary"),
        name=name,
    )(*args, *views)
    converted = [c.reshape((1,) + w.shape[1:]) for c, (w, _) in zip(res[n_out:], casts)]
    return res[:n_out], converted


def _lane_fold(v):
    out = v[:, 0:LANES]
    for c in range(1, v.shape[1] // LANES):
        out = out + v[:, c * LANES:(c + 1) * LANES]
    return out


def _rinv_from_folded(ss, d):
    total = jnp.sum(ss, axis=-1, keepdims=True)
    return jnp.broadcast_to(lax.rsqrt(total * (1.0 / d) + EPS), ss.shape)


def _row_scale(acc, rinv):
    return acc * jnp.tile(rinv, (1, acc.shape[1] // LANES))


def _norm_prep_kernel(x_ref, g_ref, xg_ref, rinv_ref):
    x = x_ref[...]
    xg_ref[...] = (x * g_ref[...]).astype(xg_ref.dtype)
    rinv_ref[...] = _rinv_from_folded(_lane_fold(x * x), x.shape[1])


def _norm_prep(x, g):
    t, d = x.shape
    tr = _tile(t, 256)
    return pl.pallas_call(
        _norm_prep_kernel,
        grid=(t // tr,),
        in_specs=[pl.BlockSpec((tr, d), lambda i: (i, 0)),
                  pl.BlockSpec((1, d), lambda i: (0, 0))],
        out_specs=[pl.BlockSpec((tr, d), lambda i: (i, 0)), pl.BlockSpec((tr, LANES), lambda i: (i, 0))],
        out_shape=[jax.ShapeDtypeStruct((t, d), BF16), jax.ShapeDtypeStruct((t, LANES), F32)],
        compiler_params=_params("arbitrary"),
        name="norm_prep",
    )(x, g.reshape(1, d))


def _mm_kernel(a_ref, rinv_ref, b_ref, o_ref):
    acc = jnp.dot(a_ref[...], b_ref[...], preferred_element_type=F32)
    o_ref[...] = _row_scale(acc, rinv_ref[...]).astype(o_ref.dtype)


def _matmul(a, rinv, b, layer, out_dtype, casts=()):
    m, k = a.shape
    n = b.shape[2]
    tm, tn = _tile(m, 1024), _tile(n, 1024)
    (out,), converted = _pallas_with_casts(
        _mm_kernel, (a, rinv, b),
        grid=(m // tm, n // tn),
        in_specs=[pl.BlockSpec((tm, k), lambda i, j: (i, 0)),
                  pl.BlockSpec((tm, LANES), lambda i, j: (i, 0)),
                  pl.BlockSpec((None, k, tn), lambda i, j: (layer, 0, j))],
        out_specs=[pl.BlockSpec((tm, tn), lambda i, j: (i, j))],
        out_shape=[jax.ShapeDtypeStruct((m, n), out_dtype)],
        name="matmul", casts=casts)
    return out, converted


def _mm_res_kernel(a_ref, b_ref, r_ref, o_ref):
    o_ref[...] = r_ref[...] + jnp.dot(a_ref[...], b_ref[...], preferred_element_type=F32)


def _mm_res_norm_kernel(a_ref, b_ref, r_ref, g_ref, o_ref, xg_ref, rinv_ref, ss_ref, *, d, rsub):
    @pl.when(pl.program_id(1) == 0)
    def _():
        ss_ref[...] = jnp.zeros_like(ss_ref)

    def project(r0):
        return r_ref[r0:r0 + rsub, :] + jnp.dot(a_ref[r0:r0 + rsub, :], b_ref[...], preferred_element_type=F32)

    def finish(r0, x):
        rows = slice(r0, r0 + rsub)
        o_ref[rows, :] = x
        xg_ref[rows, :] = (x * g_ref[...]).astype(xg_ref.dtype)
        ss = _lane_fold(x * x) + ss_ref[rows, :]
        ss_ref[rows, :] = ss
        rinv_ref[rows, :] = _rinv_from_folded(ss, d)

    pending = None
    for r0 in range(0, a_ref.shape[0], rsub):
        x = project(r0)
        if pending is not None:
            finish(*pending)
        pending = (r0, x)
    finish(*pending)


def _matmul_residual(a, b, layer, res, g_next=None, casts=()):
    m, k = a.shape
    n = b.shape[2]
    tm = _tile(m, 1024)
    tn = _tile(n, (2 * 1024 * 1024) // k)
    in_specs = [pl.BlockSpec((tm, k), lambda i, j: (i, 0)),
                pl.BlockSpec((None, k, tn), lambda i, j: (layer, 0, j)),
                pl.BlockSpec((tm, tn), lambda i, j: (i, j))]
    tile = pl.BlockSpec((tm, tn), lambda i, j: (i, j))
    if g_next is None:
        return pl.pallas_call(
            _mm_res_kernel,
            grid=(m // tm, n // tn),
            in_specs=in_specs,
            out_specs=tile,
            out_shape=jax.ShapeDtypeStruct((m, n), F32),
            compiler_params=_params("arbitrary", "arbitrary"),
            name="matmul_residual",
        )(a, b, res)
    return _pallas_with_casts(
        functools.partial(_mm_res_norm_kernel, d=n, rsub=_tile(tm, 512)), (a, b, res, g_next.reshape(1, n)),
        grid=(m // tm, n // tn),
        in_specs=in_specs + [pl.BlockSpec((1, tn), lambda i, j: (0, j))],
        out_specs=[tile, tile, pl.BlockSpec((tm, LANES), lambda i, j: (i, 0))],
        out_shape=[jax.ShapeDtypeStruct((m, n), F32), jax.ShapeDtypeStruct((m, n), BF16),
                   jax.ShapeDtypeStruct((m, LANES), F32)],
        scratch_shapes=[pltpu.VMEM((tm, LANES), F32)],
        name="matmul_residual_norm", casts=casts)


def _retention_kernel(q_ref, k_ref, v_ref, g_ref, mask_ref, qdec_ref, kdec_ref, cdec_ref, gn_ref,
                      o_ref, state_ref):
    @pl.when(pl.program_id(1) == 0)
    def _():
        state_ref[...] = jnp.zeros_like(state_ref)

    heads = range(RET_HEADS)
    q = [q_ref[:, h * RET_DK:(h + 1) * RET_DK] for h in heads]
    k = [k_ref[:, h * RET_DK:(h + 1) * RET_DK] for h in heads]
    v = [v_ref[:, h * RET_DV:(h + 1) * RET_DV] for h in heads]
    scores = [lax.dot_general(q[h], k[h], (((1,), (1,)), ((), ())), preferred_element_type=F32) * mask_ref[h]
              for h in heads]
    state = [state_ref[h] for h in heads]
    cross = [jnp.dot((q[h].astype(F32) * qdec_ref[h]).astype(BF16), state[h].astype(BF16),
                     preferred_element_type=F32) for h in heads]
    inner = [jnp.dot(scores[h].astype(BF16), v[h], preferred_element_type=F32) for h in heads]
    for h in heads:
        kd = (k[h].astype(F32) * kdec_ref[h]).astype(BF16)
        kv = lax.dot_general(kd, v[h], (((0,), (0,)), ((), ())), preferred_element_type=F32)
        state_ref[h] = state[h] * cdec_ref[h] + kv
    for h in heads:
        out = inner[h] + cross[h]
        ms = jnp.mean(out * out, axis=-1, keepdims=True)
        y = out * lax.rsqrt(ms + EPS) * gn_ref[h]
        g = g_ref[:, h * RET_DV:(h + 1) * RET_DV].astype(F32)
        o_ref[:, h * RET_DV:(h + 1) * RET_DV] = (g * jax.nn.sigmoid(g) * y).astype(o_ref.dtype)


def _retention_tables(blk):
    lg = jnp.log(1.0 - 2.0 ** (-5.0 - jnp.arange(RET_HEADS, dtype=F32)))
    pos = jnp.arange(blk, dtype=F32)
    dist = jnp.abs(pos[:, None] - pos[None, :])
    allowed = (jnp.arange(blk)[None, :] // CHUNK) <= (jnp.arange(blk)[:, None] // CHUNK)
    kscale = RET_DK ** -0.5
    mask = jnp.where(allowed[None], jnp.exp(dist[None] * lg[:, None, None]), 0.0) * kscale
    qdec = jnp.exp((pos[None, :] + 1.0) * lg[:, None])
    kdec = jnp.exp((blk - 1.0 - pos[None, :]) * lg[:, None]) * kscale
    cdec = jnp.exp(blk * lg)
    qdec = jnp.broadcast_to(qdec[:, :, None], (RET_HEADS, blk, RET_DK))
    kdec = jnp.broadcast_to(kdec[:, :, None], (RET_HEADS, blk, RET_DK))
    cdec = jnp.broadcast_to(cdec[:, None, None], (RET_HEADS, 1, RET_DV))
    return mask, qdec, kdec, cdec


def _retention(z, out_norm, bsz, seq):
    t = z.shape[0]
    blk = _tile(seq, 256)
    nblk = seq // blk
    mask, qdec, kdec, cdec = _retention_tables(blk)
    qk_w, v_w = RET_HEADS * RET_DK, RET_HEADS * RET_DV
    whole = lambda shape: pl.BlockSpec(shape, lambda b, i: (0,) * len(shape))
    return pl.pallas_call(
        _retention_kernel,
        grid=(bsz, nblk),
        in_specs=[
            pl.BlockSpec((blk, qk_w), lambda b, i: (b * nblk + i, COL_RQ // qk_w)),
            pl.BlockSpec((blk, qk_w), lambda b, i: (b * nblk + i, COL_RK // qk_w)),
            pl.BlockSpec((blk, v_w), lambda b, i: (b * nblk + i, COL_RV // v_w)),
            pl.BlockSpec((blk, v_w), lambda b, i: (b * nblk + i, COL_RG // v_w)),
            whole((RET_HEADS, blk, blk)),
            whole((RET_HEADS, blk, RET_DK)),
            whole((RET_HEADS, blk, RET_DK)),
            whole((RET_HEADS, 1, RET_DV)),
            whole((RET_HEADS, 1, RET_DV)),
        ],
        out_specs=pl.BlockSpec((blk, v_w), lambda b, i: (b * nblk + i, 0)),
        out_shape=jax.ShapeDtypeStruct((t, v_w), BF16),
        scratch_shapes=[pltpu.VMEM((RET_HEADS, RET_DK, RET_DV), F32)],
        compiler_params=_params("arbitrary", "arbitrary"),
        name="retention",
    )(z, z, z, z, mask, qdec, kdec, cdec, out_norm.reshape(RET_HEADS, 1, RET_DV))


def _qk_norm_kernel(q_ref, k_ref, v_ref, gq_ref, gk_ref, qt_ref, ko_ref, vt_ref):
    gq = gq_ref[...] * (DIFF_DH ** -0.5 * LOG2E)
    gk = gk_ref[...]
    for c in range(q_ref.shape[1] // DIFF_DH):
        cols = slice(c * DIFF_DH, (c + 1) * DIFF_DH)
        q = q_ref[:, cols].astype(F32)
        q = q * lax.rsqrt(jnp.mean(q * q, axis=-1, keepdims=True) + EPS) * gq
        qt_ref[cols, :] = q.T.astype(qt_ref.dtype)
        k = k_ref[:, cols].astype(F32)
        k = k * lax.rsqrt(jnp.mean(k * k, axis=-1, keepdims=True) + EPS) * gk
        ko_ref[:, cols] = k.astype(ko_ref.dtype)
        vt_ref[cols, :] = v_ref[:, cols].astype(F32).T.astype(vt_ref.dtype)


def _qk_norm(z, gq, gk):
    t = z.shape[0]
    w = DIFF_HEADS * 2 * DIFF_DH
    tr = _tile(t, 512)
    spec = lambda col: pl.BlockSpec((tr, w), lambda i: (i, col // w))
    vec = pl.BlockSpec((1, DIFF_DH), lambda i: (0, 0))
    rows = pl.BlockSpec((tr, w), lambda i: (i, 0))
    cols = pl.BlockSpec((w, tr), lambda i: (0, i))
    return pl.pallas_call(
        _qk_norm_kernel,
        grid=(t // tr,),
        in_specs=[spec(COL_DQ), spec(COL_DK), spec(COL_DV), vec, vec],
        out_specs=[cols, rows, cols],
        out_shape=[jax.ShapeDtypeStruct((w, t), BF16), jax.ShapeDtypeStruct((t, w), BF16),
                   jax.ShapeDtypeStruct((w, t), BF16)],
        compiler_params=_params("arbitrary"),
        name="qk_norm",
    )(z, z, z, gq.reshape(1, DIFF_DH), gk.reshape(1, DIFF_DH))


def _diff_attn_kernel(qt_ref, k_ref, vt_ref, nb_ref, slope_ref, lam_ref, gn_ref, o_ref,
                      m_sc, l_sc, acc_sc, p_sc, a_sc, *, blk, lam_init):
    i = pl.program_id(2)
    m_sc[...] = jnp.full_like(m_sc, NEG_INF)
    l_sc[...] = jnp.zeros_like(l_sc)
    acc_sc[...] = jnp.zeros_like(acc_sc)

    def scores(j):
        ks = k_ref[pl.ds(pl.multiple_of(j * blk, blk), blk), :]
        bias = nb_ref[0, (j == i).astype(jnp.int32)]
        return [jnp.dot(ks[:, m * DIFF_DH:(m + 1) * DIFF_DH], qt_ref[m * DIFF_DH:(m + 1) * DIFF_DH, :],
                        preferred_element_type=F32) + bias for m in range(2)]

    def softmax_update(j, slot, s1s):
        soff = slope_ref[0] * ((i - j) * blk).astype(F32)
        for m in range(2):
            m_old = m_sc[m]
            m_new = jnp.maximum(m_old, jnp.max(s1s[m], axis=0, keepdims=True) - soff)
            alpha = jnp.exp2(m_old - m_new)
            p = jnp.exp2((s1s[m] - (m_new + soff)).astype(BF16))
            l_sc[m] = alpha * l_sc[m] + jnp.sum(p.astype(F32), axis=0, keepdims=True)
            m_sc[m] = m_new
            p_sc[slot, m] = p
            a_sc[slot, m] = alpha

    def add_values(j, slot):
        vts = vt_ref[:, pl.ds(pl.multiple_of(j * blk, blk), blk)]
        for m in range(2):
            acc_sc[m] = a_sc[slot, m] * acc_sc[m] + jnp.dot(vts, p_sc[slot, m], preferred_element_type=F32)

    def step(j, slot):
        s1s = scores(j)
        add_values(j - 1, 1 - slot)
        softmax_update(j, slot, s1s)

    softmax_update(0, 0, scores(0))

    def pair(t, carry):
        step(2 * t + 1, 1)
        step(2 * t + 2, 0)
        return carry

    lax.fori_loop(0, i // 2, pair, 0)

    @pl.when(i % 2 == 1)
    def _():
        step(i, 1)
        add_values(i, 1)

    @pl.when(i % 2 == 0)
    def _():
        add_values(i, 0)

    lv = lam_ref[...]
    lam = (jnp.exp(jnp.sum(lv[0:1] * lv[1:2], axis=-1, keepdims=True))
           - jnp.exp(jnp.sum(lv[2:3] * lv[3:4], axis=-1, keepdims=True)) + lam_init)
    out = acc_sc[0] / l_sc[0] - lam * (acc_sc[1] / l_sc[1])
    ms = jnp.mean(out * out, axis=0, keepdims=True)
    y = out * lax.rsqrt(ms + EPS) * gn_ref[0] * (1.0 - lam_init)
    o_ref[...] = y.T.astype(o_ref.dtype)


def _diff_attention(qt, kd, vt, lam_params, out_norm, lam_init, bsz, seq):
    t = kd.shape[0]
    hw = 2 * DIFF_DH
    blk = _tile(seq, 512)
    nblk = seq // blk
    slopes = 2.0 ** (-8.0 * (jnp.arange(DIFF_HEADS, dtype=F32) + 1.0) / DIFF_HEADS) * LOG2E
    r = jnp.arange(blk)[None, :]
    c = jnp.arange(blk)[:, None]
    past = -slopes[:, None, None] * (r - c).astype(F32)[None]
    allowed = (c // CHUNK) <= (r // CHUNK)
    diag = jnp.where(allowed[None], -slopes[:, None, None] * jnp.abs(r - c).astype(F32)[None], NEG_INF)
    nb = jnp.stack([past, diag], axis=1)
    slope_rows = jnp.broadcast_to(slopes[:, None, None], (DIFF_HEADS, 1, blk))
    gn = jnp.broadcast_to(out_norm.reshape(DIFF_HEADS, hw, 1), (DIFF_HEADS, hw, blk))
    kernel = functools.partial(_diff_attn_kernel, blk=blk, lam_init=lam_init)
    return pl.pallas_call(
        kernel,
        grid=(bsz, DIFF_HEADS, nblk),
        in_specs=[
            pl.BlockSpec((hw, blk), lambda b, h, i: (h, b * nblk + i)),
            pl.BlockSpec((seq, hw), lambda b, h, i: (b, h)),
            pl.BlockSpec((hw, seq), lambda b, h, i: (h, b)),
            pl.BlockSpec((1, 2, blk, blk), lambda b, h, i: (h, 0, 0, 0)),
            pl.BlockSpec((1, 1, blk), lambda b, h, i: (h, 0, 0)),
            pl.BlockSpec((4, DIFF_DH), lambda b, h, i: (0, 0)),
            pl.BlockSpec((1, hw, blk), lambda b, h, i: (h, 0, 0)),
        ],
        out_specs=pl.BlockSpec((blk, hw), lambda b, h, i: (b * nblk + i, h)),
        out_shape=jax.ShapeDtypeStruct((t, DIFF_HEADS * hw), BF16),
        scratch_shapes=[pltpu.VMEM((2, 1, blk), F32), pltpu.VMEM((2, 1, blk), F32),
                        pltpu.VMEM((2, hw, blk), F32), pltpu.VMEM((2, 2, blk, blk), BF16),
                        pltpu.VMEM((2, 2, 1, blk), F32)],
        compiler_params=_params("arbitrary", "arbitrary", "arbitrary"),
        name="diff_attention",
    )(qt, kd, vt, nb, slope_rows, lam_params, gn)


def _conv_module_kernel(ca_ref, cb_ref, cah_ref, cbh_ref, w_ref, b_ref, g_ref, beta_ref, o_ref, xe_ref,
                        xs_ref, *, rows, sub):
    first = pl.program_id(1) == 0
    halo = cah_ref[...].astype(F32) * jax.nn.sigmoid(cbh_ref[...].astype(F32))
    xe_ref[0:CONV_HALO, :] = jnp.where(first, 0.0, halo)
    xe_ref[CONV_HALO:CONV_HALO + rows, :] = ca_ref[...].astype(F32) * jax.nn.sigmoid(cb_ref[...].astype(F32))

    base = CONV_HALO - (CONV_K - 1)
    span = xs_ref.shape[1]
    for s in range(1, 8):
        xs_ref[s - 1] = xe_ref[s:s + span, :]

    for r0 in range(0, rows, sub):
        acc = jnp.broadcast_to(b_ref[...], (sub, CONV_CH))
        for k in range(CONV_K):
            a, s = divmod(base + k, 8)
            src = xe_ref if s == 0 else xs_ref.at[s - 1]
            acc = acc + jnp.tile(w_ref[k], (sub // 8, 1)) * src[r0 + 8 * a:r0 + 8 * a + sub, :]
        mu = jnp.mean(acc, axis=-1, keepdims=True)
        xc = acc - mu
        var = jnp.mean(xc * xc, axis=-1, keepdims=True)
        y = xc * lax.rsqrt(var + EPS) * g_ref[...] + beta_ref[...]
        o_ref[r0:r0 + sub, :] = (y * jax.nn.sigmoid(y)).astype(o_ref.dtype)


def _conv_module(z, w, b, ln_g, ln_b, bsz, seq):
    t = z.shape[0]
    rows = _tile(seq, 256)
    nblk = seq // rows
    hpb = rows // CONV_HALO
    main = lambda col: pl.BlockSpec((rows, CONV_CH), lambda bb, i: (bb * nblk + i, col // CONV_CH))
    halo = lambda col: pl.BlockSpec(
        (CONV_HALO, CONV_CH), lambda bb, i: (jnp.maximum((bb * nblk + i) * hpb - 1, 0), col // CONV_CH))
    vec = pl.BlockSpec((1, CONV_CH), lambda bb, i: (0, 0))
    kernel = functools.partial(_conv_module_kernel, rows=rows, sub=32)
    return pl.pallas_call(
        kernel,
        grid=(bsz, nblk),
        in_specs=[main(COL_CA), main(COL_CB), halo(COL_CA), halo(COL_CB),
                  pl.BlockSpec((CONV_K, 8, CONV_CH), lambda bb, i: (0, 0, 0)), vec, vec, vec],
        out_specs=pl.BlockSpec((rows, CONV_CH), lambda bb, i: (bb * nblk + i, 0)),
        out_shape=jax.ShapeDtypeStruct((t, CONV_CH), BF16),
        scratch_shapes=[pltpu.VMEM((CONV_HALO + rows, CONV_CH), F32),
                        pltpu.VMEM((7, CONV_HALO + rows - 8, CONV_CH), F32)],
        compiler_params=_params("arbitrary", "arbitrary"),
        name="conv_module",
    )(z, z, z, z, jnp.broadcast_to(w[:, None, :], (CONV_K, 8, CONV_CH)), b.reshape(1, CONV_CH),
      ln_g.reshape(1, CONV_CH), ln_b.reshape(1, CONV_CH))


def _merge_kernel(h_ref, rinv_ref, wg0_ref, wg1_ref, wg2_ref, bg0_ref, bg1_ref, bg2_ref, ya_ref, yb_ref,
                  yc_ref, wbr_ref, o_ref):
    h = h_ref[...]
    rinv = jnp.tile(rinv_ref[...], (1, o_ref.shape[1] // LANES))
    merged = None
    for j, (wg_ref, bg_ref, y_ref) in enumerate(((wg0_ref, bg0_ref, ya_ref), (wg1_ref, bg1_ref, yb_ref),
                                                 (wg2_ref, bg2_ref, yc_ref))):
        gate = jax.nn.sigmoid(jnp.dot(h, wg_ref[...], preferred_element_type=F32) * rinv + bg_ref[...])
        term = gate * jnp.dot(y_ref[...], wbr_ref[j], preferred_element_type=F32)
        merged = term if merged is None else merged + term
    o_ref[...] = merged.astype(o_ref.dtype)


def _gated_merge(h, rinv, w_gate, b_gate, ya, yb, yc, w_br, layer, casts=()):
    t, d = h.shape
    tm, tn = _tile(t, 512), _tile(d, 512)
    nd = d // tn
    ysp = pl.BlockSpec((tm, BRANCH_WIDTH), lambda n, m: (m, 0))
    wsp = [pl.BlockSpec((None, d, tn), lambda n, m, j=j: (layer, 0, j * nd + n)) for j in range(N_BRANCH)]
    bsp = [pl.BlockSpec((1, tn), lambda n, m, j=j: (0, j * nd + n)) for j in range(N_BRANCH)]
    (out,), converted = _pallas_with_casts(
        _merge_kernel, (h, rinv, w_gate, w_gate, w_gate, b_gate, b_gate, b_gate, ya, yb, yc, w_br),
        grid=(nd, t // tm),
        in_specs=[pl.BlockSpec((tm, d), lambda n, m: (m, 0)), pl.BlockSpec((tm, LANES), lambda n, m: (m, 0)),
                  *wsp, *bsp, ysp, ysp, ysp,
                  pl.BlockSpec((None, N_BRANCH, BRANCH_WIDTH, tn), lambda n, m: (layer, 0, 0, n))],
        out_specs=[pl.BlockSpec((tm, tn), lambda n, m: (m, n))],
        out_shape=[jax.ShapeDtypeStruct((t, d), BF16)],
        name="gated_merge", casts=casts)
    return out, converted


def _ffn_in_kernel(h_ref, rinv_ref, wg_ref, wu_ref, cw_ref, cb_ref, o_ref, fg_ref, *, tm, tn, rsub, csub,
                   tiles_per_seq):
    seq_start = pl.program_id(1) % tiles_per_seq == 0

    @pl.when(seq_start)
    def _():
        fg_ref[0:FFN_HALO, :] = jnp.zeros((FFN_HALO, tn), F32)

    @pl.when(jnp.logical_not(seq_start))
    def _():
        fg_ref[0:FFN_HALO, :] = fg_ref[tm:tm + FFN_HALO, :]

    def project(r0, c0):
        h = h_ref[r0:r0 + rsub, :]
        rinv = rinv_ref[r0:r0 + rsub, :]
        fg = _row_scale(jnp.dot(h, wg_ref[:, c0:c0 + csub], preferred_element_type=F32), rinv)
        fu = _row_scale(jnp.dot(h, wu_ref[:, c0:c0 + csub], preferred_element_type=F32), rinv)
        fg_ref[FFN_HALO + r0:FFN_HALO + r0 + rsub, c0:c0 + csub] = fg
        return fg, fu

    def finish(r0, c0, fg, fu):
        cols = slice(c0, c0 + csub)
        conv = (cw_ref[0:1, cols] * fg_ref[FFN_HALO - 2 + r0:FFN_HALO - 2 + r0 + rsub, cols]
                + cw_ref[1:2, cols] * fg_ref[FFN_HALO - 1 + r0:FFN_HALO - 1 + r0 + rsub, cols]
                + cw_ref[2:3, cols] * fg + cb_ref[:, cols])
        gelu = 0.5 * conv * (1.0 + lax.erf(conv * (2.0 ** -0.5)))
        o_ref[r0:r0 + rsub, cols] = (gelu * fu).astype(o_ref.dtype)

    pending = None
    for r0 in range(0, tm, rsub):
        for c0 in range(0, tn, csub):
            projected = project(r0, c0)
            if pending is not None:
                finish(*pending)
            pending = (r0, c0, *projected)
    finish(*pending)


def _ffn_in(h, rinv, w_in, layer, conv_w, conv_b, seq, casts=()):
    t, d = h.shape
    f = w_in.shape[2] // 2
    tm, tn = _tile(seq, 1024), _tile(f, 512)
    rsub, csub = _tile(tm, FFN_ROW_SUB), _tile(tn, FFN_COL_SUB)
    nf = f // tn
    kernel = functools.partial(_ffn_in_kernel, tm=tm, tn=tn, rsub=rsub, csub=csub, tiles_per_seq=seq // tm)
    (out,), converted = _pallas_with_casts(
        kernel, (h, rinv, w_in, w_in, conv_w, conv_b.reshape(1, f)),
        grid=(nf, t // tm),
        in_specs=[pl.BlockSpec((tm, d), lambda n, m: (m, 0)),
                  pl.BlockSpec((tm, LANES), lambda n, m: (m, 0)),
                  pl.BlockSpec((None, d, tn), lambda n, m: (layer, 0, n)),
                  pl.BlockSpec((None, d, tn), lambda n, m: (layer, 0, nf + n)),
                  pl.BlockSpec((FFN_CONV_K, tn), lambda n, m: (0, n)),
                  pl.BlockSpec((1, tn), lambda n, m: (0, n))],
        out_specs=[pl.BlockSpec((tm, tn), lambda n, m: (m, n))],
        out_shape=[jax.ShapeDtypeStruct((t, f), BF16)],
        scratch_shapes=[pltpu.VMEM((FFN_HALO + tm, tn), F32)],
        name="ffn_in", casts=casts)
    return out, converted


def kernel(x, norm_mix, w_in, w_gate, b_gate, diff_q_norm, diff_k_norm, diff_lambda, diff_out_norm,
           ret_out_norm, conv_w, conv_b, conv_ln_g, conv_ln_b, w_br, w_o, norm_ffn, w_ffn_in,
           ffn_conv_w, ffn_conv_b, w_ffn_out):
    bsz, seq, d = x.shape
    depth = w_in.shape[0]
    t = bsz * seq
    x = x.reshape(t, d)
    wi, wg, wb, wo = (w[0:1].astype(BF16) for w in (w_in, w_gate, w_br, w_o))
    h, rinv = _norm_prep(x, norm_mix[0])
    for l in range(depth):
        lam_init = 0.8 - 0.6 * math.exp(-0.3 * l)
        nxt = l + 1 < depth
        z, (wfi,) = _matmul(h, rinv, wi, 0, BF16, casts=((w_ffn_in, l),))
        ya = _retention(z, ret_out_norm[l], bsz, seq)
        qt, kd, vt = _qk_norm(z, diff_q_norm[l], diff_k_norm[l])
        yb = _diff_attention(qt, kd, vt, diff_lambda[l], diff_out_norm[l], lam_init, bsz, seq)
        yc = _conv_module(z, conv_w[l], conv_b[l], conv_ln_g[l], conv_ln_b[l], bsz, seq)
        merged, (wfo,) = _gated_merge(h, rinv, wg, b_gate[l].reshape(1, N_BRANCH * d), ya, yb, yc, wb, 0,
                                      casts=((w_ffn_out, l),))
        (x, h, rinv), _ = _matmul_residual(merged, wo, 0, x, norm_ffn[l])
        act, nxt_in = _ffn_in(h, rinv, wfi, 0, ffn_conv_w[l], ffn_conv_b[l], seq,
                              casts=((w_in, l + 1), (w_o, l + 1)) if nxt else ())
        if nxt:
            (x, h, rinv), (wg, wb) = _matmul_residual(act, wfo, 0, x, norm_mix[l + 1],
                                                      casts=((w_gate, l + 1), (w_br, l + 1)))
            wi, wo = nxt_in
        else:
            x = _matmul_residual(act, wfo, 0, x)
    return x.reshape(bsz, seq, d)
```

```python
import functools
import math

import jax
import jax.numpy as jnp
from jax import lax
from jax.experimental import pallas as pl
from jax.experimental.pallas import tpu as pltpu

EPS = 1e-6
NEG_INF = -1e30
CHUNK = 64
LOG2E = 1.4426950408889634

RET_HEADS = 4
RET_DK = 128
RET_DV = 256
DIFF_HEADS = 4
DIFF_DH = 128
CONV_CH = 1024
CONV_K = 31
FFN_CONV_K = 3
BRANCH_WIDTH = 1024
N_BRANCH = 3

COL_RQ = 0
COL_RK = COL_RQ + RET_HEADS * RET_DK
COL_RV = COL_RK + RET_HEADS * RET_DK
COL_RG = COL_RV + RET_HEADS * RET_DV
COL_DQ = COL_RG + RET_HEADS * RET_DV
COL_DK = COL_DQ + DIFF_HEADS * 2 * DIFF_DH
COL_DV = COL_DK + DIFF_HEADS * 2 * DIFF_DH
COL_CA = COL_DV + DIFF_HEADS * 2 * DIFF_DH
COL_CB = COL_CA + CONV_CH

LANES = 128
CAST_ROWS = 16
VMEM_LIMIT_BYTES = 56 * 1024 * 1024
CONV_HALO = 32
FFN_HALO = 8
FFN_ROW_SUB = 512
FFN_COL_SUB = 512

F32 = jnp.float32
BF16 = jnp.bfloat16


def _params(*sem):
    return pltpu.CompilerParams(dimension_semantics=sem, vmem_limit_bytes=VMEM_LIMIT_BYTES)


def _tile(n, pref):
    t = min(n, pref)
    assert n % t == 0, (n, pref)
    return t


def _pallas_with_casts(kernel, args, *, grid, in_specs, out_specs, out_shape, name, scratch_shapes=(), casts=()):
    out_specs, out_shape = list(out_specs), list(out_shape)
    n_in, n_out, n_cast = len(in_specs), len(out_shape), len(casts)
    inner = grid[1]
    steps = grid[0] * grid[1]
    views, slab_in, slab_out = [], [], []
    for w, layer in casts:
        v = w.reshape(w.shape[0], -1, w.shape[-1])
        rows, cols = v.shape[1:]
        slab = CAST_ROWS * pl.cdiv(rows, CAST_ROWS * steps)
        assert rows % slab == 0, (w.shape, grid)
        last = rows // slab - 1
        views.append(v)
        slab_in.append(pl.BlockSpec((None, slab, cols), lambda i, j, layer=layer, last=last:
                                    (layer, jnp.minimum(i * inner + j, last), 0)))
        slab_out.append(pl.BlockSpec((slab, cols), lambda i, j, last=last: (jnp.minimum(i * inner + j, last), 0)))

    def body(*refs):
        ins, refs = refs[:n_in], refs[n_in:]
        cast_in, refs = refs[:n_cast], refs[n_cast:]
        outs, refs = refs[:n_out], refs[n_out:]
        cast_out, scratch = refs[:n_cast], refs[n_cast:]
        kernel(*ins, *outs, *scratch)
        for src, dst in zip(cast_in, cast_out):
            dst[...] = src[...].astype(dst.dtype)

    res = pl.pallas_call(
        body,
        grid=grid,
        in_specs=[*in_specs, *slab_in],
        out_specs=[*out_specs, *slab_out],
        out_shape=[*out_shape, *[jax.ShapeDtypeStruct(v.shape[1:], BF16) for v in views]],
        scratch_shapes=list(scratch_shapes),
        compiler_params=_params("arbitrary", "arbitrary"),
        name=name,
    )(*args, *views)
    converted = [c.reshape((1,) + w.shape[1:]) for c, (w, _) in zip(res[n_out:], casts)]
    return res[:n_out], converted


def _lane_fold(v):
    out = v[:, 0:LANES]
    for c in range(1, v.shape[1] // LANES):
        out = out + v[:, c * LANES:(c + 1) * LANES]
    return out


def _rinv_from_folded(ss, d):
    total = jnp.sum(ss, axis=-1, keepdims=True)
    return jnp.broadcast_to(lax.rsqrt(total * (1.0 / d) + EPS), ss.shape)


def _row_scale(acc, rinv):
    return acc * jnp.tile(rinv, (1, acc.shape[1] // LANES))


def _norm_prep_kernel(x_ref, g_ref, xg_ref, rinv_ref):
    x = x_ref[...]
    xg_ref[...] = (x * g_ref[...]).astype(xg_ref.dtype)
    rinv_ref[...] = _rinv_from_folded(_lane_fold(x * x), x.shape[1])


def _norm_prep(x, g):
    t, d = x.shape
    tr = _tile(t, 256)
    return pl.pallas_call(
        _norm_prep_kernel,
        grid=(t // tr,),
        in_specs=[pl.BlockSpec((tr, d), lambda i: (i, 0)),
                  pl.BlockSpec((1, d), lambda i: (0, 0))],
        out_specs=[pl.BlockSpec((tr, d), lambda i: (i, 0)), pl.BlockSpec((tr, LANES), lambda i: (i, 0))],
        out_shape=[jax.ShapeDtypeStruct((t, d), BF16), jax.ShapeDtypeStruct((t, LANES), F32)],
        compiler_params=_params("arbitrary"),
        name="norm_prep",
    )(x, g.reshape(1, d))


def _mm_kernel(a_ref, rinv_ref, b_ref, o_ref):
    acc = jnp.dot(a_ref[...], b_ref[...], preferred_element_type=F32)
    o_ref[...] = _row_scale(acc, rinv_ref[...]).astype(o_ref.dtype)


def _matmul(a, rinv, b, layer, out_dtype, casts=()):
    m, k = a.shape
    n = b.shape[2]
    tm, tn = _tile(m, 1024), _tile(n, 1024)
    (out,), converted = _pallas_with_casts(
        _mm_kernel, (a, rinv, b),
        grid=(m // tm, n // tn),
        in_specs=[pl.BlockSpec((tm, k), lambda i, j: (i, 0)),
                  pl.BlockSpec((tm, LANES), lambda i, j: (i, 0)),
                  pl.BlockSpec((None, k, tn), lambda i, j: (layer, 0, j))],
        out_specs=[pl.BlockSpec((tm, tn), lambda i, j: (i, j))],
        out_shape=[jax.ShapeDtypeStruct((m, n), out_dtype)],
        name="matmul", casts=casts)
    return out, converted


def _mm_res_kernel(a_ref, b_ref, r_ref, o_ref):
    o_ref[...] = r_ref[...] + jnp.dot(a_ref[...], b_ref[...], preferred_element_type=F32)


def _mm_res_norm_kernel(a_ref, b_ref, r_ref, g_ref, o_ref, xg_ref, rinv_ref, ss_ref, *, d, rsub):
    @pl.when(pl.program_id(1) == 0)
    def _():
        ss_ref[...] = jnp.zeros_like(ss_ref)

    def project(r0):
        return r_ref[r0:r0 + rsub, :] + jnp.dot(a_ref[r0:r0 + rsub, :], b_ref[...], preferred_element_type=F32)

    def finish(r0, x):
        rows = slice(r0, r0 + rsub)
        o_ref[rows, :] = x
        xg_ref[rows, :] = (x * g_ref[...]).astype(xg_ref.dtype)
        ss = _lane_fold(x * x) + ss_ref[rows, :]
        ss_ref[rows, :] = ss
        rinv_ref[rows, :] = _rinv_from_folded(ss, d)

    pending = None
    for r0 in range(0, a_ref.shape[0], rsub):
        x = project(r0)
        if pending is not None:
            finish(*pending)
        pending = (r0, x)
    finish(*pending)


def _matmul_residual(a, b, layer, res, g_next=None, casts=()):
    m, k = a.shape
    n = b.shape[2]
    tm = _tile(m, 1024)
    tn = _tile(n, (2 * 1024 * 1024) // k)
    in_specs = [pl.BlockSpec((tm, k), lambda i, j: (i, 0)),
                pl.BlockSpec((None, k, tn), lambda i, j: (layer, 0, j)),
                pl.BlockSpec((tm, tn), lambda i, j: (i, j))]
    tile = pl.BlockSpec((tm, tn), lambda i, j: (i, j))
    if g_next is None:
        return pl.pallas_call(
            _mm_res_kernel,
            grid=(m // tm, n // tn),
            in_specs=in_specs,
            out_specs=tile,
            out_shape=jax.ShapeDtypeStruct((m, n), F32),
            compiler_params=_params("arbitrary", "arbitrary"),
            name="matmul_residual",
        )(a, b, res)
    return _pallas_with_casts(
        functools.partial(_mm_res_norm_kernel, d=n, rsub=_tile(tm, 512)), (a, b, res, g_next.reshape(1, n)),
        grid=(m // tm, n // tn),
        in_specs=in_specs + [pl.BlockSpec((1, tn), lambda i, j: (0, j))],
        out_specs=[tile, tile, pl.BlockSpec((tm, LANES), lambda i, j: (i, 0))],
        out_shape=[jax.ShapeDtypeStruct((m, n), F32), jax.ShapeDtypeStruct((m, n), BF16),
                   jax.ShapeDtypeStruct((m, LANES), F32)],
        scratch_shapes=[pltpu.VMEM((tm, LANES), F32)],
        name="matmul_residual_norm", casts=casts)


def _retention_kernel(q_ref, k_ref, v_ref, g_ref, mask_ref, qdec_ref, kdec_ref, cdec_ref, gn_ref,
                      dq_ref, dk_ref, dv_ref, gq_ref, gk_ref, o_ref, qt_ref, ko_ref, vt_ref, state_ref):
    @pl.when(pl.program_id(1) == 0)
    def _():
        state_ref[...] = jnp.zeros_like(state_ref)

    _qk_norm_kernel(dq_ref, dk_ref, dv_ref, gq_ref, gk_ref, qt_ref, ko_ref, vt_ref)

    heads = range(RET_HEADS)
    q = [q_ref[:, h * RET_DK:(h + 1) * RET_DK] for h in heads]
    k = [k_ref[:, h * RET_DK:(h + 1) * RET_DK] for h in heads]
    v = [v_ref[:, h * RET_DV:(h + 1) * RET_DV] for h in heads]
    scores = [lax.dot_general(q[h], k[h], (((1,), (1,)), ((), ())), preferred_element_type=F32) * mask_ref[h]
              for h in heads]
    state = [state_ref[h] for h in heads]
    cross = [jnp.dot((q[h].astype(F32) * qdec_ref[h]).astype(BF16), state[h].astype(BF16),
                     preferred_element_type=F32) for h in heads]
    inner = [jnp.dot(scores[h].astype(BF16), v[h], preferred_element_type=F32) for h in heads]
    for h in heads:
        kd = (k[h].astype(F32) * kdec_ref[h]).astype(BF16)
        kv = lax.dot_general(kd, v[h], (((0,), (0,)), ((), ())), preferred_element_type=F32)
        state_ref[h] = state[h] * cdec_ref[h] + kv
    for h in heads:
        out = inner[h] + cross[h]
        ms = jnp.mean(out * out, axis=-1, keepdims=True)
        y = out * lax.rsqrt(ms + EPS) * gn_ref[h]
        g = g_ref[:, h * RET_DV:(h + 1) * RET_DV].astype(F32)
        o_ref[:, h * RET_DV:(h + 1) * RET_DV] = (g * jax.nn.sigmoid(g) * y).astype(o_ref.dtype)


def _retention_tables(blk):
    lg = jnp.log(1.0 - 2.0 ** (-5.0 - jnp.arange(RET_HEADS, dtype=F32)))
    pos = jnp.arange(blk, dtype=F32)
    dist = jnp.abs(pos[:, None] - pos[None, :])
    allowed = (jnp.arange(blk)[None, :] // CHUNK) <= (jnp.arange(blk)[:, None] // CHUNK)
    kscale = RET_DK ** -0.5
    mask = jnp.where(allowed[None], jnp.exp(dist[None] * lg[:, None, None]), 0.0) * kscale
    qdec = jnp.exp((pos[None, :] + 1.0) * lg[:, None])
    kdec = jnp.exp((blk - 1.0 - pos[None, :]) * lg[:, None]) * kscale
    cdec = jnp.exp(blk * lg)
    qdec = jnp.broadcast_to(qdec[:, :, None], (RET_HEADS, blk, RET_DK))
    kdec = jnp.broadcast_to(kdec[:, :, None], (RET_HEADS, blk, RET_DK))
    cdec = jnp.broadcast_to(cdec[:, None, None], (RET_HEADS, 1, RET_DV))
    return mask, qdec, kdec, cdec


def _retention(z, out_norm, gq, gk, bsz, seq):
    t = z.shape[0]
    blk = _tile(seq, 256)
    nblk = seq // blk
    mask, qdec, kdec, cdec = _retention_tables(blk)
    qk_w, v_w = RET_HEADS * RET_DK, RET_HEADS * RET_DV
    dw = DIFF_HEADS * 2 * DIFF_DH
    whole = lambda shape: pl.BlockSpec(shape, lambda b, i: (0,) * len(shape))
    rows = lambda col: pl.BlockSpec((blk, dw), lambda b, i: (b * nblk + i, col // dw))
    cols = pl.BlockSpec((dw, blk), lambda b, i: (0, b * nblk + i))
    return pl.pallas_call(
        _retention_kernel,
        grid=(bsz, nblk),
        in_specs=[
            pl.BlockSpec((blk, qk_w), lambda b, i: (b * nblk + i, COL_RQ // qk_w)),
            pl.BlockSpec((blk, qk_w), lambda b, i: (b * nblk + i, COL_RK // qk_w)),
            pl.BlockSpec((blk, v_w), lambda b, i: (b * nblk + i, COL_RV // v_w)),
            pl.BlockSpec((blk, v_w), lambda b, i: (b * nblk + i, COL_RG // v_w)),
            whole((RET_HEADS, blk, blk)),
            whole((RET_HEADS, blk, RET_DK)),
            whole((RET_HEADS, blk, RET_DK)),
            whole((RET_HEADS, 1, RET_DV)),
            whole((RET_HEADS, 1, RET_DV)),
            rows(COL_DQ), rows(COL_DK), rows(COL_DV), whole((1, DIFF_DH)), whole((1, DIFF_DH)),
        ],
        out_specs=[pl.BlockSpec((blk, v_w), lambda b, i: (b * nblk + i, 0)), cols,
                   pl.BlockSpec((blk, dw), lambda b, i: (b * nblk + i, 0)), cols],
        out_shape=[jax.ShapeDtypeStruct((t, v_w), BF16), jax.ShapeDtypeStruct((dw, t), BF16),
                   jax.ShapeDtypeStruct((t, dw), BF16), jax.ShapeDtypeStruct((dw, t), BF16)],
        scratch_shapes=[pltpu.VMEM((RET_HEADS, RET_DK, RET_DV), F32)],
        compiler_params=_params("arbitrary", "arbitrary"),
        name="retention",
    )(z, z, z, z, mask, qdec, kdec, cdec, out_norm.reshape(RET_HEADS, 1, RET_DV),
      z, z, z, gq.reshape(1, DIFF_DH), gk.reshape(1, DIFF_DH))


def _qk_norm_kernel(q_ref, k_ref, v_ref, gq_ref, gk_ref, qt_ref, ko_ref, vt_ref):
    gq = gq_ref[...] * (DIFF_DH ** -0.5 * LOG2E)
    gk = gk_ref[...]
    for c in range(q_ref.shape[1] // DIFF_DH):
        cols = slice(c * DIFF_DH, (c + 1) * DIFF_DH)
        q = q_ref[:, cols].astype(F32)
        q = q * lax.rsqrt(jnp.mean(q * q, axis=-1, keepdims=True) + EPS) * gq
        qt_ref[cols, :] = q.T.astype(qt_ref.dtype)
        k = k_ref[:, cols].astype(F32)
        k = k * lax.rsqrt(jnp.mean(k * k, axis=-1, keepdims=True) + EPS) * gk
        ko_ref[:, cols] = k.astype(ko_ref.dtype)
        vt_ref[cols, :] = v_ref[:, cols].astype(F32).T.astype(vt_ref.dtype)


def _diff_attn_kernel(qt_ref, k_ref, vt_ref, nb_ref, slope_ref, lam_ref, gn_ref, o_ref,
                      m_sc, l_sc, acc_sc, p_sc, a_sc, *, blk, lam_init):
    i = pl.program_id(2)
    m_sc[...] = jnp.full_like(m_sc, NEG_INF)
    l_sc[...] = jnp.zeros_like(l_sc)
    acc_sc[...] = jnp.zeros_like(acc_sc)

    def scores(j):
        ks = k_ref[pl.ds(pl.multiple_of(j * blk, blk), blk), :]
        bias = nb_ref[0, (j == i).astype(jnp.int32)]
        return [jnp.dot(ks[:, m * DIFF_DH:(m + 1) * DIFF_DH], qt_ref[m * DIFF_DH:(m + 1) * DIFF_DH, :],
                        preferred_element_type=F32) + bias for m in range(2)]

    def softmax_update(j, slot, s1s):
        soff = slope_ref[0] * ((i - j) * blk).astype(F32)
        for m in range(2):
            m_old = m_sc[m]
            m_new = jnp.maximum(m_old, jnp.max(s1s[m], axis=0, keepdims=True) - soff)
            alpha = jnp.exp2(m_old - m_new)
            p = jnp.exp2(s1s[m] - (m_new + soff))
            l_sc[m] = alpha * l_sc[m] + jnp.sum(p, axis=0, keepdims=True)
            m_sc[m] = m_new
            p_sc[slot, m] = p.astype(BF16)
            a_sc[slot, m] = alpha

    def add_values(j, slot):
        vts = vt_ref[:, pl.ds(pl.multiple_of(j * blk, blk), blk)]
        for m in range(2):
            acc_sc[m] = a_sc[slot, m] * acc_sc[m] + jnp.dot(vts, p_sc[slot, m], preferred_element_type=F32)

    def step(j, slot):
        s1s = scores(j)
        add_values(j - 1, 1 - slot)
        softmax_update(j, slot, s1s)

    softmax_update(0, 0, scores(0))

    def pair(t, carry):
        step(2 * t + 1, 1)
        step(2 * t + 2, 0)
        return carry

    lax.fori_loop(0, i // 2, pair, 0)

    @pl.when(i % 2 == 1)
    def _():
        step(i, 1)
        add_values(i, 1)

    @pl.when(i % 2 == 0)
    def _():
        add_values(i, 0)

    lv = lam_ref[...]
    lam = (jnp.exp(jnp.sum(lv[0:1] * lv[1:2], axis=-1, keepdims=True))
           - jnp.exp(jnp.sum(lv[2:3] * lv[3:4], axis=-1, keepdims=True)) + lam_init)
    out = acc_sc[0] / l_sc[0] - lam * (acc_sc[1] / l_sc[1])
    ms = jnp.mean(out * out, axis=0, keepdims=True)
    y = out * lax.rsqrt(ms + EPS) * gn_ref[0] * (1.0 - lam_init)
    o_ref[...] = y.T.astype(o_ref.dtype)


def _diff_attention(qt, kd, vt, lam_params, out_norm, lam_init, bsz, seq):
    t = kd.shape[0]
    hw = 2 * DIFF_DH
    blk = _tile(seq, 512)
    nblk = seq // blk
    slopes = 2.0 ** (-8.0 * (jnp.arange(DIFF_HEADS, dtype=F32) + 1.0) / DIFF_HEADS) * LOG2E
    r = jnp.arange(blk)[None, :]
    c = jnp.arange(blk)[:, None]
    past = -slopes[:, None, None] * (r - c).astype(F32)[None]
    allowed = (c // CHUNK) <= (r // CHUNK)
    diag = jnp.where(allowed[None], -slopes[:, None, None] * jnp.abs(r - c).astype(F32)[None], NEG_INF)
    nb = jnp.stack([past, diag], axis=1)
    slope_rows = jnp.broadcast_to(slopes[:, None, None], (DIFF_HEADS, 1, blk))
    gn = jnp.broadcast_to(out_norm.reshape(DIFF_HEADS, hw, 1), (DIFF_HEADS, hw, blk))
    kernel = functools.partial(_diff_attn_kernel, blk=blk, lam_init=lam_init)
    return pl.pallas_call(
        kernel,
        grid=(bsz, DIFF_HEADS, nblk),
        in_specs=[
            pl.BlockSpec((hw, blk), lambda b, h, i: (h, b * nblk + i)),
            pl.BlockSpec((seq, hw), lambda b, h, i: (b, h)),
            pl.BlockSpec((hw, seq), lambda b, h, i: (h, b)),
            pl.BlockSpec((1, 2, blk, blk), lambda b, h, i: (h, 0, 0, 0)),
            pl.BlockSpec((1, 1, blk), lambda b, h, i: (h, 0, 0)),
            pl.BlockSpec((4, DIFF_DH), lambda b, h, i: (0, 0)),
            pl.BlockSpec((1, hw, blk), lambda b, h, i: (h, 0, 0)),
        ],
        out_specs=pl.BlockSpec((blk, hw), lambda b, h, i: (b * nblk + i, h)),
        out_shape=jax.ShapeDtypeStruct((t, DIFF_HEADS * hw), BF16),
        scratch_shapes=[pltpu.VMEM((2, 1, blk), F32), pltpu.VMEM((2, 1, blk), F32),
                        pltpu.VMEM((2, hw, blk), F32), pltpu.VMEM((2, 2, blk, blk), BF16),
                        pltpu.VMEM((2, 2, 1, blk), F32)],
        compiler_params=_params("arbitrary", "arbitrary", "arbitrary"),
        name="diff_attention",
    )(qt, kd, vt, nb, slope_rows, lam_params, gn)


def _conv_module_kernel(ca_ref, cb_ref, cah_ref, cbh_ref, w_ref, b_ref, g_ref, beta_ref, o_ref, xe_ref,
                        xs_ref, *, rows, sub):
    first = pl.program_id(1) == 0
    halo = cah_ref[...].astype(F32) * jax.nn.sigmoid(cbh_ref[...].astype(F32))
    xe_ref[0:CONV_HALO, :] = jnp.where(first, 0.0, halo)
    xe_ref[CONV_HALO:CONV_HALO + rows, :] = ca_ref[...].astype(F32) * jax.nn.sigmoid(cb_ref[...].astype(F32))

    base = CONV_HALO - (CONV_K - 1)
    span = xs_ref.shape[1]
    for s in range(1, 8):
        xs_ref[s - 1] = xe_ref[s:s + span, :]

    for r0 in range(0, rows, sub):
        acc = jnp.broadcast_to(b_ref[...], (sub, CONV_CH))
        for k in range(CONV_K):
            a, s = divmod(base + k, 8)
            src = xe_ref if s == 0 else xs_ref.at[s - 1]
            acc = acc + jnp.tile(w_ref[k], (sub // 8, 1)) * src[r0 + 8 * a:r0 + 8 * a + sub, :]
        mu = jnp.mean(acc, axis=-1, keepdims=True)
        xc = acc - mu
        var = jnp.mean(xc * xc, axis=-1, keepdims=True)
        y = xc * lax.rsqrt(var + EPS) * g_ref[...] + beta_ref[...]
        o_ref[r0:r0 + sub, :] = (y * jax.nn.sigmoid(y)).astype(o_ref.dtype)


def _conv_module(z, w, b, ln_g, ln_b, bsz, seq):
    t = z.shape[0]
    rows = _tile(seq, 256)
    nblk = seq // rows
    hpb = rows // CONV_HALO
    main = lambda col: pl.BlockSpec((rows, CONV_CH), lambda bb, i: (bb * nblk + i, col // CONV_CH))
    halo = lambda col: pl.BlockSpec(
        (CONV_HALO, CONV_CH), lambda bb, i: (jnp.maximum((bb * nblk + i) * hpb - 1, 0), col // CONV_CH))
    vec = pl.BlockSpec((1, CONV_CH), lambda bb, i: (0, 0))
    kernel = functools.partial(_conv_module_kernel, rows=rows, sub=32)
    return pl.pallas_call(
        kernel,
        grid=(bsz, nblk),
        in_specs=[main(COL_CA), main(COL_CB), halo(COL_CA), halo(COL_CB),
                  pl.BlockSpec((CONV_K, 8, CONV_CH), lambda bb, i: (0, 0, 0)), vec, vec, vec],
        out_specs=pl.BlockSpec((rows, CONV_CH), lambda bb, i: (bb * nblk + i, 0)),
        out_shape=jax.ShapeDtypeStruct((t, CONV_CH), BF16),
        scratch_shapes=[pltpu.VMEM((CONV_HALO + rows, CONV_CH), F32),
                        pltpu.VMEM((7, CONV_HALO + rows - 8, CONV_CH), F32)],
        compiler_params=_params("arbitrary", "arbitrary"),
        name="conv_module",
    )(z, z, z, z, jnp.broadcast_to(w[:, None, :], (CONV_K, 8, CONV_CH)), b.reshape(1, CONV_CH),
      ln_g.reshape(1, CONV_CH), ln_b.reshape(1, CONV_CH))


def _merge_kernel(h_ref, rinv_ref, wg0_ref, wg1_ref, wg2_ref, bg0_ref, bg1_ref, bg2_ref, ya_ref, yb_ref,
                  yc_ref, wbr_ref, o_ref):
    h = h_ref[...]
    rinv = jnp.tile(rinv_ref[...], (1, o_ref.shape[1] // LANES))
    merged = None
    for j, (wg_ref, bg_ref, y_ref) in enumerate(((wg0_ref, bg0_ref, ya_ref), (wg1_ref, bg1_ref, yb_ref),
                                                 (wg2_ref, bg2_ref, yc_ref))):
        gate = jax.nn.sigmoid(jnp.dot(h, wg_ref[...], preferred_element_type=F32) * rinv + bg_ref[...])
        term = gate * jnp.dot(y_ref[...], wbr_ref[j], preferred_element_type=F32)
        merged = term if merged is None else merged + term
    o_ref[...] = merged.astype(o_ref.dtype)


def _gated_merge(h, rinv, w_gate, b_gate, ya, yb, yc, w_br, layer, casts=()):
    t, d = h.shape
    tm, tn = _tile(t, 512), _tile(d, 512)
    nd = d // tn
    ysp = pl.BlockSpec((tm, BRANCH_WIDTH), lambda n, m: (m, 0))
    wsp = [pl.BlockSpec((None, d, tn), lambda n, m, j=j: (layer, 0, j * nd + n)) for j in range(N_BRANCH)]
    bsp = [pl.BlockSpec((1, tn), lambda n, m, j=j: (0, j * nd + n)) for j in range(N_BRANCH)]
    (out,), converted = _pallas_with_casts(
        _merge_kernel, (h, rinv, w_gate, w_gate, w_gate, b_gate, b_gate, b_gate, ya, yb, yc, w_br),
        grid=(nd, t // tm),
        in_specs=[pl.BlockSpec((tm, d), lambda n, m: (m, 0)), pl.BlockSpec((tm, LANES), lambda n, m: (m, 0)),
                  *wsp, *bsp, ysp, ysp, ysp,
                  pl.BlockSpec((None, N_BRANCH, BRANCH_WIDTH, tn), lambda n, m: (layer, 0, 0, n))],
        out_specs=[pl.BlockSpec((tm, tn), lambda n, m: (m, n))],
        out_shape=[jax.ShapeDtypeStruct((t, d), BF16)],
        name="gated_merge", casts=casts)
    return out, converted


def _ffn_in_kernel(h_ref, rinv_ref, wg_ref, wu_ref, cw_ref, cb_ref, o_ref, fg_ref, *, tm, tn, rsub, csub,
                   tiles_per_seq):
    seq_start = pl.program_id(1) % tiles_per_seq == 0

    @pl.when(seq_start)
    def _():
        fg_ref[0:FFN_HALO, :] = jnp.zeros((FFN_HALO, tn), F32)

    @pl.when(jnp.logical_not(seq_start))
    def _():
        fg_ref[0:FFN_HALO, :] = fg_ref[tm:tm + FFN_HALO, :]

    def project(r0, c0):
        h = h_ref[r0:r0 + rsub, :]
        rinv = rinv_ref[r0:r0 + rsub, :]
        fg = _row_scale(jnp.dot(h, wg_ref[:, c0:c0 + csub], preferred_element_type=F32), rinv)
        fu = _row_scale(jnp.dot(h, wu_ref[:, c0:c0 + csub], preferred_element_type=F32), rinv)
        fg_ref[FFN_HALO + r0:FFN_HALO + r0 + rsub, c0:c0 + csub] = fg
        return fg, fu

    def finish(r0, c0, fg, fu):
        cols = slice(c0, c0 + csub)
        conv = (cw_ref[0:1, cols] * fg_ref[FFN_HALO - 2 + r0:FFN_HALO - 2 + r0 + rsub, cols]
                + cw_ref[1:2, cols] * fg_ref[FFN_HALO - 1 + r0:FFN_HALO - 1 + r0 + rsub, cols]
                + cw_ref[2:3, cols] * fg + cb_ref[:, cols])
        gelu = 0.5 * conv * (1.0 + lax.erf(conv * (2.0 ** -0.5)))
        o_ref[r0:r0 + rsub, cols] = (gelu * fu).astype(o_ref.dtype)

    pending = None
    for r0 in range(0, tm, rsub):
        for c0 in range(0, tn, csub):
            projected = project(r0, c0)
            if pending is not None:
                finish(*pending)
            pending = (r0, c0, *projected)
    finish(*pending)


def _ffn_in(h, rinv, w_in, layer, conv_w, conv_b, seq, casts=()):
    t, d = h.shape
    f = w_in.shape[2] // 2
    tm, tn = _tile(seq, 1024), _tile(f, 512)
    rsub, csub = _tile(tm, FFN_ROW_SUB), _tile(tn, FFN_COL_SUB)
    nf = f // tn
    kernel = functools.partial(_ffn_in_kernel, tm=tm, tn=tn, rsub=rsub, csub=csub, tiles_per_seq=seq // tm)
    (out,), converted = _pallas_with_casts(
        kernel, (h, rinv, w_in, w_in, conv_w, conv_b.reshape(1, f)),
        grid=(nf, t // tm),
        in_specs=[pl.BlockSpec((tm, d), lambda n, m: (m, 0)),
                  pl.BlockSpec((tm, LANES), lambda n, m: (m, 0)),
                  pl.BlockSpec((None, d, tn), lambda n, m: (layer, 0, n)),
                  pl.BlockSpec((None, d, tn), lambda n, m: (layer, 0, nf + n)),
                  pl.BlockSpec((FFN_CONV_K, tn), lambda n, m: (0, n)),
                  pl.BlockSpec((1, tn), lambda n, m: (0, n))],
        out_specs=[pl.BlockSpec((tm, tn), lambda n, m: (m, n))],
        out_shape=[jax.ShapeDtypeStruct((t, f), BF16)],
        scratch_shapes=[pltpu.VMEM((FFN_HALO + tm, tn), F32)],
        name="ffn_in", casts=casts)
    return out, converted


def kernel(x, norm_mix, w_in, w_gate, b_gate, diff_q_norm, diff_k_norm, diff_lambda, diff_out_norm,
           ret_out_norm, conv_w, conv_b, conv_ln_g, conv_ln_b, w_br, w_o, norm_ffn, w_ffn_in,
           ffn_conv_w, ffn_conv_b, w_ffn_out):
    bsz, seq, d = x.shape
    depth = w_in.shape[0]
    t = bsz * seq
    x = x.reshape(t, d)
    wi, wg, wb, wo = (w[0:1].astype(BF16) for w in (w_in, w_gate, w_br, w_o))
    h, rinv = _norm_prep(x, norm_mix[0])
    for l in range(depth):
        lam_init = 0.8 - 0.6 * math.exp(-0.3 * l)
        nxt = l + 1 < depth
        z, (wfi,) = _matmul(h, rinv, wi, 0, BF16, casts=((w_ffn_in, l),))
        ya, qt, kd, vt = _retention(z, ret_out_norm[l], diff_q_norm[l], diff_k_norm[l], bsz, seq)
        yb = _diff_attention(qt, kd, vt, diff_lambda[l], diff_out_norm[l], lam_init, bsz, seq)
        yc = _conv_module(z, conv_w[l], conv_b[l], conv_ln_g[l], conv_ln_b[l], bsz, seq)
        merged, (wfo,) = _gated_merge(h, rinv, wg, b_gate[l].reshape(1, N_BRANCH * d), ya, yb, yc, wb, 0,
                                      casts=((w_ffn_out, l),))
        (x, h, rinv), _ = _matmul_residual(merged, wo, 0, x, norm_ffn[l])
        act, nxt_in = _ffn_in(h, rinv, wfi, 0, ffn_conv_w[l], ffn_conv_b[l], seq,
                              casts=((w_in, l + 1), (w_o, l + 1)) if nxt else ())
        if nxt:
            (x, h, rinv), (wg, wb) = _matmul_residual(act, wfo, 0, x, norm_mix[l + 1],
                                                      casts=((w_gate, l + 1), (w_br, l + 1)))
            wi, wo = nxt_in
        else:
            x = _matmul_residual(act, wfo, 0, x)
    return x.reshape(bsz, seq, d)
```
